```python
import jax
import jax.numpy as jnp
from jax import lax
import numpy as np

D_MODEL = 1024
BATCH = 8
SEQ = 4096
DEPTH = 2

GRID_W = 64
CTX_LEN = 256
HEAD_DIM = 64
BRANCH_WIDTH = 512
N_BRANCH = 3
SGU_GROUPS = 4
SGU_GROUP_DIM = BRANCH_WIDTH // SGU_GROUPS
SGU_CHUNK = 128
SWA_Q_HEADS = BRANCH_WIDTH // HEAD_DIM
SWA_KV_HEADS = 2
SWA_GROUP = SWA_Q_HEADS // SWA_KV_HEADS
SWA_WINDOW = 128
SWA_BLOCK = 128
ROPE_BASE = 10000.0
GLA_HEADS = 4
GLA_DK = 64
GLA_DV = BRANCH_WIDTH // GLA_HEADS
GLA_RANK = 16
GLA_NORMALIZER = 16.0
GLA_CHUNK = 64
FFN_DIM = 4 * D_MODEL
EPS = 1e-6

IN_SPLITS = (
    BRANCH_WIDTH, BRANCH_WIDTH,
    SWA_Q_HEADS * HEAD_DIM, SWA_KV_HEADS * HEAD_DIM, SWA_KV_HEADS * HEAD_DIM,
    GLA_HEADS * GLA_DK, GLA_HEADS * GLA_DK, BRANCH_WIDTH, BRANCH_WIDTH,
    GLA_RANK, GLA_RANK,
    N_BRANCH * D_MODEL,
)
IN_DIM = sum(IN_SPLITS)

kernel_name = 'hybrid_gated_dit_block'


def rms_norm(x, gain=None):
    xf = x.astype(jnp.float32)
    y = xf * lax.rsqrt(jnp.mean(xf * xf, axis=-1, keepdims=True) + EPS)
    if gain is not None:
        y = y * gain.astype(jnp.float32)
    return y.astype(x.dtype)


def adaln(cond, w_ada, b_ada):
    return jnp.split(jax.nn.silu(cond) @ w_ada + b_ada, 6, axis=-1)


def modulate(x, shift, scale):
    return rms_norm(x) * (1.0 + scale) + shift


def project(h, w_in):
    idx = np.cumsum(IN_SPLITS)[:-1].tolist()
    return jnp.split(h @ w_in, idx, axis=-1)


def heads(t, dim):
    return t.reshape(*t.shape[:-1], -1, dim)


def axial_rope(x, row, col):
    n_freq = HEAD_DIM // 4
    freqs = ROPE_BASE ** (-jnp.arange(n_freq, dtype=jnp.float32) / n_freq)

    def rotate(xa, pos):
        ang = pos.astype(jnp.float32)[:, None] * freqs[None, :]
        cos = jnp.cos(ang)[None, :, None, :]
        sin = jnp.sin(ang)[None, :, None, :]
        x1, x2 = xa[..., :n_freq], xa[..., n_freq:]
        return jnp.concatenate([x1 * cos - x2 * sin, x1 * sin + x2 * cos], axis=-1)

    xf = x.astype(jnp.float32)
    half = HEAD_DIM // 2
    out = jnp.concatenate([rotate(xf[..., :half], row), rotate(xf[..., half:], col)], axis=-1)
    return out.astype(x.dtype)


def sgu_mix(u, v, gain, w_s, b_s):
    B, L, _ = u.shape
    nc = L // SGU_CHUNK
    u = jax.nn.gelu(u)
    v = rms_norm(jax.nn.gelu(v).reshape(B, L, SGU_GROUPS, SGU_GROUP_DIM),
                 gain.reshape(SGU_GROUPS, SGU_GROUP_DIM))
    v = v.reshape(B, nc, SGU_CHUNK, SGU_GROUPS, SGU_GROUP_DIM)
    s = jnp.einsum('gpq,bcqgd->bcpgd', w_s, v) + b_s.T[None, None, :, :, None]
    return u * s.reshape(B, L, BRANCH_WIDTH)


def swa_latent(q, k, v, k_ctx, v_ctx, sink):
    B, N = q.shape[0], q.shape[1]
    n_blk = N // SWA_BLOCK
    pad = ((0, 0), (SWA_BLOCK, SWA_BLOCK), (0, 0), (0, 0))
    k_pad = jnp.pad(k, pad)
    v_pad = jnp.pad(v, pad)
    scale = HEAD_DIM ** -0.5
    sink_l = jnp.broadcast_to(sink.reshape(SWA_KV_HEADS, SWA_GROUP).astype(jnp.float32)[None, :, :, None, None],
                              (B, SWA_KV_HEADS, SWA_GROUP, SWA_BLOCK, 1))
    rel = (jnp.arange(3 * SWA_BLOCK) - SWA_BLOCK)[None, :] - jnp.arange(SWA_BLOCK)[:, None]
    n_loc = 3 * SWA_BLOCK

    def one_block(i):
        start = i * SWA_BLOCK
        qb = lax.dynamic_slice_in_dim(q, start, SWA_BLOCK, axis=1)
        kb = lax.dynamic_slice_in_dim(k_pad, start, n_loc, axis=1)
        vb = lax.dynamic_slice_in_dim(v_pad, start, n_loc, axis=1)
        kpos = start - SWA_BLOCK + jnp.arange(n_loc)
        valid = (jnp.abs(rel) <= SWA_WINDOW) & ((kpos >= 0) & (kpos < N))[None, :]
        s_loc = jnp.einsum('bqhgd,bkhd->bhgqk', qb, kb).astype(jnp.float32) * scale
        s_loc = jnp.where(valid, s_loc, -jnp.inf)
        s_ctx = jnp.einsum('bqhgd,bkhd->bhgqk', qb, k_ctx).astype(jnp.float32) * scale
        p = jax.nn.softmax(jnp.concatenate([s_loc, s_ctx, sink_l], axis=-1), axis=-1).astype(v.dtype)
        o = jnp.einsum('bhgqk,bkhd->bqhgd', p[..., :n_loc], vb)
        o = o + jnp.einsum('bhgqk,bkhd->bqhgd', p[..., n_loc:n_loc + k_ctx.shape[1]], v_ctx)
        return o

    out = lax.map(one_block, jnp.arange(n_blk))
    return jnp.moveaxis(out, 0, 1).reshape(B, N, BRANCH_WIDTH)


def ctx_attention(q, k, v, sink):
    B, C = q.shape[0], q.shape[1]
    s = jnp.einsum('bqhgd,bkhd->bhgqk', q, k).astype(jnp.float32) * HEAD_DIM ** -0.5
    s_sink = jnp.broadcast_to(sink.reshape(SWA_KV_HEADS, SWA_GROUP).astype(jnp.float32)[None, :, :, None, None],
                              (B, SWA_KV_HEADS, SWA_GROUP, C, 1))
    p = jax.nn.softmax(jnp.concatenate([s, s_sink], axis=-1), axis=-1)[..., :-1].astype(v.dtype)
    return jnp.einsum('bhgqk,bkhd->bqhgd', p, v).reshape(B, C, BRANCH_WIDTH)


def gla_log_decay(z, w2, b2):
    logit = (z @ w2 + b2).astype(jnp.float32)
    return (jax.nn.log_sigmoid(logit) / GLA_NORMALIZER).reshape(*z.shape[:-1], GLA_HEADS, GLA_DK)


def gla_chunked(q, k, v, g, s0):
    B, L, H, DK = q.shape
    DV = v.shape[-1]
    nc = L // GLA_CHUNK

    def chunks(t):
        return t.astype(jnp.float32).reshape(B, nc, GLA_CHUNK, H, t.shape[-1])

    q = chunks(q) * GLA_DK ** -0.5
    k, v, g = chunks(k), chunks(v), chunks(g)
    b = jnp.cumsum(g, axis=2)
    b_ref = b[:, :, GLA_CHUNK // 2][:, :, None]
    b_tot = b[:, :, -1]
    scores = jnp.einsum('bclhd,bcmhd->bchlm', q * jnp.exp(b - b_ref), k * jnp.exp(b_ref - b))
    lower = jnp.tril(jnp.ones((GLA_CHUNK, GLA_CHUNK), dtype=bool))
    scores = jnp.where(lower, scores, 0.0)
    o_intra = jnp.einsum('bchlm,bcmhv->bclhv', scores, v)
    q_in = q * jnp.exp(b)
    k_out = k * jnp.exp(b_tot[:, :, None] - b)

    def step(state, inp):
        q_c, k_c, v_c, bt = inp
        o_c = jnp.einsum('blhd,bhdv->blhv', q_c, state)
        state = state * jnp.exp(bt)[..., None] + jnp.einsum('blhd,blhv->bhdv', k_c, v_c)
        return state, o_c

    xs = (jnp.moveaxis(q_in, 1, 0), jnp.moveaxis(k_out, 1, 0), jnp.moveaxis(v, 1, 0), jnp.moveaxis(b_tot, 1, 0))
    s_fin, o_inter = lax.scan(step, s0.astype(jnp.float32), xs)
    o = o_intra + jnp.moveaxis(o_inter, 0, 1)
    return o.reshape(B, L, H, DV), s_fin


def gla_bidir(q, k, v, g_f, g_b, s_f, s_b):
    o_f, sf = gla_chunked(q, k, v, g_f, s_f)
    o_b, sb = gla_chunked(jnp.flip(q, 1), jnp.flip(k, 1), jnp.flip(v, 1), jnp.flip(g_b, 1), s_b)
    return o_f + jnp.flip(o_b, 1), sf, sb


def gla_out(o, r, gain):
    return rms_norm(o, gain).astype(r.dtype).reshape(r.shape) * jax.nn.silu(r)


def merge_branches(a, b, cc, gates, w_br, w_o):
    br = jnp.stack([a, b, cc], axis=-2)
    proj = jnp.einsum('blkw,kwd->blkd', br, w_br)
    gate = jax.nn.sigmoid(gates.reshape(*gates.shape[:-1], N_BRANCH, D_MODEL))
    return jnp.sum(gate * proj, axis=-2) @ w_o


def sq_relu_ffn(h, w1, w2):
    return jnp.square(jax.nn.relu(h @ w1)) @ w2


def setup_inputs(seed: int = 0) -> dict:
    key = jax.random.key(seed)
    ks = jax.random.split(key, 24)

    def nrm(k, shape, scale):
        return jax.random.normal(k, shape, jnp.float32) * scale

    def gain(k, shape):
        return 1.0 + 0.02 * jax.random.normal(k, shape, jnp.float32)

    return {
        'x': nrm(ks[0], (BATCH, SEQ, D_MODEL), 1.0),
        'c': nrm(ks[1], (BATCH, D_MODEL), 1.0),
        'ctx': nrm(ks[2], (BATCH, CTX_LEN, D_MODEL), 1.0),
        'c_ctx': nrm(ks[3], (D_MODEL,), 1.0),
        'w_ada': nrm(ks[4], (DEPTH, D_MODEL, 6 * D_MODEL), D_MODEL ** -0.5),
        'b_ada': nrm(ks[5], (DEPTH, 6 * D_MODEL), 0.01),
        'w_in': nrm(ks[6], (DEPTH, D_MODEL, IN_DIM), D_MODEL ** -0.5),
        'q_norm': gain(ks[7], (DEPTH, HEAD_DIM)),
        'k_norm': gain(ks[8], (DEPTH, HEAD_DIM)),
        'sink': nrm(ks[9], (DEPTH, SWA_Q_HEADS), 1.0),
        'sgu_norm': gain(ks[10], (DEPTH, BRANCH_WIDTH)),
        'w_sgu': nrm(ks[11], (DEPTH, SGU_GROUPS, SGU_CHUNK, SGU_CHUNK), SGU_CHUNK ** -0.5),
        'b_sgu': 1.0 + nrm(ks[12], (DEPTH, SGU_GROUPS, SGU_CHUNK), 0.01),
        'w_gate_f': nrm(ks[13], (DEPTH, GLA_RANK, GLA_HEADS * GLA_DK), GLA_RANK ** -0.5),
        'b_gate_f': nrm(ks[14], (DEPTH, GLA_HEADS * GLA_DK), 0.01),
        'w_gate_b': nrm(ks[15], (DEPTH, GLA_RANK, GLA_HEADS * GLA_DK), GLA_RANK ** -0.5),
        'b_gate_b': nrm(ks[16], (DEPTH, GLA_HEADS * GLA_DK), 0.01),
        'gla_norm': gain(ks[17], (DEPTH, GLA_DV)),
        'w_br': nrm(ks[18], (DEPTH, N_BRANCH, BRANCH_WIDTH, D_MODEL), BRANCH_WIDTH ** -0.5),
        'w_o': nrm(ks[19], (DEPTH, D_MODEL, D_MODEL), D_MODEL ** -0.5),
        'w_ff1': nrm(ks[20], (DEPTH, D_MODEL, FFN_DIM), D_MODEL ** -0.5),
        'w_ff2': nrm(ks[21], (DEPTH, FFN_DIM, D_MODEL), FFN_DIM ** -0.5),
    }


def reference(x, c, ctx, c_ctx, w_ada, b_ada, w_in, q_norm, k_norm, sink, sgu_norm, w_sgu, b_sgu,
              w_gate_f, b_gate_f, w_gate_b, b_gate_b, gla_norm, w_br, w_o, w_ff1, w_ff2):
    B, N, _ = x.shape
    C = ctx.shape[1]
    rows = N // GRID_W
    row = jnp.repeat(jnp.arange(rows), GRID_W)
    col = jnp.tile(jnp.arange(GRID_W), rows)
    zero_state = jnp.zeros((B, GLA_HEADS, GLA_DK, GLA_DV), jnp.float32)

    for l in range(DEPTH):
        sh1, sc1, g1, sh2, sc2, g2 = [m[:, None, :] for m in adaln(c, w_ada[l], b_ada[l])]
        csh1, csc1, cg1, csh2, csc2, cg2 = adaln(c_ctx, w_ada[l], b_ada[l])

        (cu, cva, cqb, ckb, cvb, cqc, ckc, cvc, crc, czf, czb, cgates) = project(modulate(ctx, csh1, csc1), w_in[l])
        ck_att = rms_norm(heads(ckb, HEAD_DIM), k_norm[l])
        cv_att = heads(cvb, HEAD_DIM)
        oc, s_fwd, s_bwd = gla_bidir(heads(cqc, GLA_DK), heads(ckc, GLA_DK), heads(cvc, GLA_DV),
                                     gla_log_decay(czf, w_gate_f[l], b_gate_f[l]),
                                     gla_log_decay(czb, w_gate_b[l], b_gate_b[l]),
                                     zero_state, zero_state)

        (u, va, qb, kb, vb, qc, kc, vc, rc, zf, zb, gates) = project(modulate(x, sh1, sc1), w_in[l])
        a_out = sgu_mix(u, va, sgu_norm[l], w_sgu[l], b_sgu[l])
        q_att = axial_rope(rms_norm(heads(qb, HEAD_DIM), q_norm[l]), row, col)
        q_att = q_att.reshape(B, N, SWA_KV_HEADS, SWA_GROUP, HEAD_DIM)
        k_att = axial_rope(rms_norm(heads(kb, HEAD_DIM), k_norm[l]), row, col)
        b_out = swa_latent(q_att, k_att, heads(vb, HEAD_DIM), ck_att, cv_att, sink[l])
        o_lat, _, _ = gla_bidir(heads(qc, GLA_DK), heads(kc, GLA_DK), heads(vc, GLA_DV),
                                gla_log_decay(zf, w_gate_f[l], b_gate_f[l]),
                                gla_log_decay(zb, w_gate_b[l], b_gate_b[l]),
                                s_fwd, s_bwd)
        c_out = gla_out(o_lat, rc, gla_norm[l])
        x_new = x + g1 * merge_branches(a_out, b_out, c_out, gates, w_br[l], w_o[l])
        x_new = x_new + g2 * sq_relu_ffn(modulate(x_new, sh2, sc2), w_ff1[l], w_ff2[l])

        if l < DEPTH - 1:
            ca_out = sgu_mix(cu, cva, sgu_norm[l], w_sgu[l], b_sgu[l])
            cq_att = rms_norm(heads(cqb, HEAD_DIM), q_norm[l]).reshape(B, C, SWA_KV_HEADS, SWA_GROUP, HEAD_DIM)
            cb_out = ctx_attention(cq_att, ck_att, cv_att, sink[l])
            cc_out = gla_out(oc, crc, gla_norm[l])
            ctx = ctx + cg1 * merge_branches(ca_out, cb_out, cc_out, cgates, w_br[l], w_o[l])
            ctx = ctx + cg2 * sq_relu_ffn(modulate(ctx, csh2, csc2), w_ff1[l], w_ff2[l])
        x = x_new

    return x
```

```python
import functools

import jax
import jax.numpy as jnp
import numpy as np
from jax import lax
from jax.experimental import pallas as pl
from jax.experimental.pallas import tpu as pltpu

D_MODEL = 1024
GRID_W = 64
HEAD_DIM = 64
BRANCH_WIDTH = 512
N_BRANCH = 3
SGU_GROUPS = 4
SGU_CHUNK = 128
SWA_Q_HEADS = 8
SWA_KV_HEADS = 2
SWA_GROUP = 4
SWA_WINDOW = 128
SWA_BLOCK = 128
ROPE_BASE = 10000.0
GLA_HEADS = 4
GLA_DK = 64
GLA_DV = 128
GLA_RANK = 16
GLA_NORMALIZER = 16.0
GLA_CHUNK = 64
FFN_DIM = 4 * D_MODEL
EPS = 1e-6

LANES = 128
VMEM_LIMIT = 56 * 1024 * 1024
NEG_BIG = -1e30
BF16 = jnp.bfloat16
F32 = jnp.float32

_SPLITS = (512, 512, 512, 128, 128, 256, 256, 512, 512, 16, 16, 3072)
_OFFS = np.concatenate([[0], np.cumsum(_SPLITS)]).tolist()


def _dot(a, b):
    return jnp.dot(a, b, preferred_element_type=F32)


def _dot_nt(a, b):
    return lax.dot_general(a, b, (((1,), (1,)), ((), ())), preferred_element_type=F32)


def _dot_tn(a, b):
    return lax.dot_general(a, b, (((0,), (0,)), ((), ())), preferred_element_type=F32)


def _split_dot(a, m):
    hi = a.astype(BF16)
    lo = (a - hi.astype(F32)).astype(BF16)
    return _dot(hi, m) + _dot(lo, m)


def _const_spec(shape):
    nd = len(shape)
    return pl.BlockSpec(shape, lambda *_: (0,) * nd, pipeline_mode=pl.Buffered(1))


def _params(sem):
    return pltpu.CompilerParams(dimension_semantics=sem, vmem_limit_bytes=VMEM_LIMIT)


def _adaln_kernel(cond_ref, w_ref, b_ref, o_ref):
    cond = cond_ref[...]
    s = (cond * jax.nn.sigmoid(cond)).astype(BF16)
    o_ref[0] = _dot(s, w_ref[0].astype(BF16)) + b_ref[0]


def _adaln(cond, w_ada, b_ada):
    depth = w_ada.shape[0]
    rows = cond.shape[0]
    tn = 1536
    return pl.pallas_call(
        _adaln_kernel,
        grid=(depth, 6 * D_MODEL // tn),
        in_specs=[
            pl.BlockSpec((rows, D_MODEL), lambda l, j: (0, 0)),
            pl.BlockSpec((1, D_MODEL, tn), lambda l, j: (l, 0, j)),
            pl.BlockSpec((1, 1, tn), lambda l, j: (l, 0, j)),
        ],
        out_specs=pl.BlockSpec((1, rows, tn), lambda l, j: (l, 0, j)),
        out_shape=jax.ShapeDtypeStruct((depth, rows, 6 * D_MODEL), F32),
        compiler_params=_params(("arbitrary", "arbitrary")),
        name="adaln",
    )(cond, w_ada, b_ada.reshape(depth, 1, 6 * D_MODEL))


def _modulated(x, shift, scale):
    ms = jnp.mean(x * x, axis=-1, keepdims=True)
    return x * lax.rsqrt(ms + EPS) * (1.0 + scale) + shift


def _proj_kernel(x_ref, sh_ref, sc_ref, cos_ref, sa_ref, sb_ref,
                 wuv_ref, wb_ref, wc_ref, wz_ref, wg_ref, w2_ref, b2_ref,
                 sgun_ref, ws_ref, bs_ref, qn_ref, kn_ref, kqs_ref,
                 a_ref, q_ref, k_ref, v_ref, kq_ref, vc_ref, r_ref, g2_ref, gates_ref):
    tm = x_ref.shape[1]
    hb = _modulated(x_ref[0], sh_ref[0], sc_ref[0]).astype(BF16)

    uv = _dot(hb, wuv_ref[...])
    u = jax.nn.gelu(uv[:, :BRANCH_WIDTH])
    va = jax.nn.gelu(uv[:, BRANCH_WIDTH:])
    for g in range(SGU_GROUPS):
        lanes = slice(g * LANES, (g + 1) * LANES)
        vg = va[:, lanes]
        vg = vg * lax.rsqrt(jnp.mean(vg * vg, axis=-1, keepdims=True) + EPS) * sgun_ref[:, lanes]
        vgb = vg.astype(BF16)
        for c in range(tm // SGU_CHUNK):
            rows = slice(c * SGU_CHUNK, (c + 1) * SGU_CHUNK)
            s = _dot(ws_ref[g], vgb[rows]) + bs_ref[g]
            a_ref[0, rows, lanes] = (u[rows, lanes] * s).astype(a_ref.dtype)

    bq = _dot(hb, wb_ref[...])
    ri = lax.broadcasted_iota(jnp.int32, (LANES, LANES), 0) // HEAD_DIM
    ci = lax.broadcasted_iota(jnp.int32, (LANES, LANES), 1) // HEAD_DIM
    group_ones = jnp.where(ri == ci, 1.0, 0.0).astype(BF16)
    cos, sa, sb = cos_ref[...], sa_ref[...], sb_ref[...]

    def norm_rope(xs, gain):
        ss = _split_dot(xs * xs, group_ones)
        y = xs * lax.rsqrt(ss * (1.0 / HEAD_DIM) + EPS) * gain
        return y * cos + pltpu.roll(y, 16, 1) * sa + pltpu.roll(y, LANES - 16, 1) * sb

    for s in range(BRANCH_WIDTH // LANES):
        lanes = slice(s * LANES, (s + 1) * LANES)
        q_ref[0, :, lanes] = (norm_rope(bq[:, lanes], qn_ref[...]) * HEAD_DIM ** -0.5).astype(q_ref.dtype)
    k_ref[0] = norm_rope(bq[:, 512:640], kn_ref[...]).astype(k_ref.dtype)
    v_ref[0] = bq[:, 640:768].astype(v_ref.dtype)

    cp = _dot(hb, wc_ref[...])
    kq_ref[0] = cp[:, :512] * kqs_ref[...]
    vc_ref[0] = cp[:, 512:1024].astype(vc_ref.dtype)
    r_ref[0] = cp[:, 1024:1536].astype(r_ref.dtype)
    z = _dot(hb, wz_ref[...]).astype(BF16)
    logit = _dot(z, w2_ref[...]) + b2_ref[...]
    log_sig = jnp.minimum(logit, 0.0) - jnp.log1p(jnp.exp(-jnp.abs(logit)))
    g2_ref[0] = log_sig * (1.0 / GLA_NORMALIZER)

    gates_ref[0] = jax.nn.sigmoid(_dot(hb, wg_ref[...])).astype(gates_ref.dtype)


def _proj(x, shift, scale, rope, lw, tm):
    B, N, _ = x.shape
    cos, sa, sb = rope
    tok = lambda w: pl.BlockSpec((1, tm, w), lambda b, i: (b, i, 0))
    mod = pl.BlockSpec((1, 1, D_MODEL), lambda b, i: (b, 0, 0))
    tab = pl.BlockSpec((tm, LANES), lambda b, i: (i, 0))
    consts = [lw["w_uv"], lw["w_b"], lw["w_c"], lw["w_z"], lw["w_g"], lw["w2"], lw["b2"],
              lw["sgu_norm"], lw["w_sgu"], lw["b_sgu"], lw["q_norm"], lw["k_norm"], lw["kq_scale"]]
    out_w = [(512, BF16), (512, BF16), (128, BF16), (128, BF16), (512, F32), (512, BF16),
             (512, BF16), (1024, F32), (3072, BF16)]
    return pl.pallas_call(
        _proj_kernel,
        grid=(B, N // tm),
        in_specs=[tok(D_MODEL), mod, mod, tab, tab, tab] + [_const_spec(c.shape) for c in consts],
        out_specs=[tok(w) for w, _ in out_w],
        out_shape=[jax.ShapeDtypeStruct((B, N, w), dt) for w, dt in out_w],
        compiler_params=_params(("parallel", "parallel")),
        name="proj",
    )(x, shift, scale, cos, sa, sb, *consts)


def _attn_kernel(sink_ref, q_ref, *refs, local, n_tokens):
    if local:
        kp_ref, kc_ref, kn_ref, vp_ref, vc_ref, vn_ref, kx_ref, vx_ref, o_ref = refs
        keys = jnp.concatenate([kp_ref[0], kc_ref[0], kn_ref[0], kx_ref[0]], axis=0)
        vals = jnp.concatenate([vp_ref[0], vc_ref[0], vn_ref[0], vx_ref[0]], axis=0)
    else:
        kx_ref, vx_ref, o_ref = refs
        keys, vals = kx_ref[0], vx_ref[0]
    n_keys = keys.shape[0]
    blk = q_ref.shape[1]
    n_slab = BRANCH_WIDTH // LANES
    rows = n_slab * blk
    q = q_ref[0]
    qs = jnp.concatenate([q[:, s * LANES:(s + 1) * LANES] for s in range(n_slab)], axis=0)
    lane = lax.broadcasted_iota(jnp.int32, (rows, LANES), 1)
    rid = lax.broadcasted_iota(jnp.int32, (rows, 1), 0)

    if local:
        i = pl.program_id(1)
        qi = lax.broadcasted_iota(jnp.int32, (rows, n_keys), 0) % blk
        kj = lax.broadcasted_iota(jnp.int32, (rows, n_keys), 1)
        rel = kj - blk - qi
        kpos = (i - 1) * blk + kj
        valid = (jnp.abs(rel) <= SWA_WINDOW) & (kpos >= 0) & (kpos < n_tokens)
        valid = valid | (kj >= 3 * blk)

    outs = []
    for h in range(SWA_KV_HEADS):
        head_lanes = (lane // HEAD_DIM) == h
        qh = jnp.where(head_lanes, qs, jnp.zeros_like(qs))
        s = _dot_nt(qh, keys)
        if local:
            s = jnp.where(valid, s, NEG_BIG)
        sink = jnp.zeros((rows, 1), F32)
        for g in range(SWA_GROUP):
            sink = jnp.where(rid // blk == g, sink_ref[h * SWA_GROUP + g], sink)
        m = jnp.maximum(jnp.max(s, axis=-1, keepdims=True), sink)
        p = jnp.exp(s - m)
        denom = jnp.sum(p, axis=-1, keepdims=True) + jnp.exp(sink - m)
        outs.append(_dot(p.astype(BF16), vals) / denom)
    o = jnp.where((lane // HEAD_DIM) == 0, outs[0], outs[1])
    for s in range(n_slab):
        o_ref[0, :, s * LANES:(s + 1) * LANES] = o[s * blk:(s + 1) * blk].astype(o_ref.dtype)


def _attn(q, k, v, k_ctx, v_ctx, sink, local):
    B, N, _ = q.shape
    C = k_ctx.shape[1]
    blk = SWA_BLOCK
    nb = N // blk
    qspec = pl.BlockSpec((1, blk, BRANCH_WIDTH), lambda b, i: (b, i, 0))
    ctx_spec = pl.BlockSpec((1, C, LANES), lambda b, i: (b, 0, 0))
    smem = pl.BlockSpec(memory_space=pltpu.SMEM)
    if local:
        prev = pl.BlockSpec((1, blk, LANES), lambda b, i: (b, jnp.maximum(i - 1, 0), 0))
        cur = pl.BlockSpec((1, blk, LANES), lambda b, i: (b, i, 0))
        nxt = pl.BlockSpec((1, blk, LANES), lambda b, i: (b, jnp.minimum(i + 1, nb - 1), 0))
        in_specs = [smem, qspec, prev, cur, nxt, prev, cur, nxt, ctx_spec, ctx_spec]
        args = (sink, q, k, k, k, v, v, v, k_ctx, v_ctx)
    else:
        in_specs = [smem, qspec, ctx_spec, ctx_spec]
        args = (sink, q, k_ctx, v_ctx)
    return pl.pallas_call(
        functools.partial(_attn_kernel, local=local, n_tokens=N),
        grid=(B, nb),
        in_specs=in_specs,
        out_specs=qspec,
        out_shape=jax.ShapeDtypeStruct((B, N, BRANCH_WIDTH), BF16),
        compiler_params=_params(("parallel", "parallel")),
        name="attn_local" if local else "attn_ctx",
    )(*args)


def _gla_kernel(kqf_ref, vf_ref, gf_ref, kqb_ref, vb_ref, gb_ref, s0_ref,
                of_ref, ob_ref, sfin_ref, st_ref):
    i = pl.program_id(1)
    T = kqf_ref.shape[1]
    n_chunk = T // GLA_CHUNK

    @pl.when(i == 0)
    def _():
        st_ref[...] = s0_ref[0]

    ri = lax.broadcasted_iota(jnp.int32, (T, T), 0)
    ci = lax.broadcasted_iota(jnp.int32, (T, T), 1)
    same_chunk = (ri // GLA_CHUNK) == (ci // GLA_CHUNK)
    li = lax.broadcasted_iota(jnp.int32, (GLA_CHUNK, GLA_CHUNK), 0)
    mi = lax.broadcasted_iota(jnp.int32, (GLA_CHUNK, GLA_CHUNK), 1)
    lane = lax.broadcasted_iota(jnp.int32, (GLA_CHUNK, LANES), 1)
    k_lanes = lane < GLA_DK

    for d, (kq_ref, v_ref, g_ref, o_ref) in enumerate(
            ((kqf_ref, vf_ref, gf_ref, of_ref), (kqb_ref, vb_ref, gb_ref, ob_ref))):
        backward = d == 1
        tri_full = jnp.where(same_chunk & ((ci >= ri) if backward else (ci <= ri)), 1.0, 0.0).astype(BF16)
        cum = _split_dot_left(tri_full, g_ref[0])
        keep = (mi >= li) if backward else (mi <= li)
        ref_row = GLA_CHUNK // 2 - 1 if backward else GLA_CHUNK // 2
        tot_row = 0 if backward else GLA_CHUNK - 1
        for h in range(GLA_HEADS):
            lanes = slice(h * LANES, (h + 1) * LANES)
            st = st_ref[d, h]
            for c in (range(n_chunk - 1, -1, -1) if backward else range(n_chunk)):
                rows = slice(c * GLA_CHUNK, (c + 1) * GLA_CHUNK)
                bc = cum[rows, lanes]
                x = kq_ref[0, rows, lanes]
                vch = v_ref[0, rows, lanes]
                b_ref = bc[ref_row:ref_row + 1]
                b_tot = bc[tot_row:tot_row + 1]
                xe1 = x * jnp.exp(jnp.where(k_lanes, b_ref - bc, bc - b_ref))
                xe2 = x * jnp.exp(jnp.where(k_lanes, b_tot - bc, bc))
                qe = jnp.where(k_lanes, pltpu.roll(xe1, GLA_DK, 1), 0.0).astype(BF16)
                scores = _dot_nt(qe, xe1.astype(BF16))
                pm = jnp.where(keep, scores, 0.0).astype(BF16)
                q_in = pltpu.roll(xe2, GLA_DK, 1).astype(BF16)
                o = _dot(pm, vch) + _dot_nt(q_in, st.astype(BF16))
                o_ref[0, rows, lanes] = o
                k_out = jnp.where(k_lanes, xe2, 0.0).astype(BF16)
                st = st * jnp.exp(b_tot) + _dot_tn(vch, k_out)
            st_ref[d, h] = st

    @pl.when(i == pl.num_programs(1) - 1)
    def _():
        sfin_ref[0] = st_ref[...]


def _split_dot_left(m, a):
    hi = a.astype(BF16)
    lo = (a - hi.astype(F32)).astype(BF16)
    return _dot(m, hi) + _dot(m, lo)


def _gla(kq, v, g2, s0, T):
    B, N, _ = kq.shape
    nt = N // T
    fwd = lambda w: pl.BlockSpec((1, T, w), lambda b, i: (b, i, 0))
    bwd = lambda w: pl.BlockSpec((1, T, w), lambda b, i: (b, nt - 1 - i, 0))
    g_fwd = pl.BlockSpec((1, T, 512), lambda b, i: (b, i, 0))
    g_bwd = pl.BlockSpec((1, T, 512), lambda b, i: (b, nt - 1 - i, 1))
    st_shape = (2, GLA_HEADS, GLA_DV, LANES)
    st_spec = pl.BlockSpec((1,) + st_shape, lambda b, i: (b, 0, 0, 0, 0))
    return pl.pallas_call(
        _gla_kernel,
        grid=(B, nt),
        in_specs=[fwd(512), fwd(512), g_fwd, bwd(512), bwd(512), g_bwd, st_spec],
        out_specs=[fwd(512), bwd(512), st_spec],
        out_shape=[jax.ShapeDtypeStruct((B, N, 512), F32), jax.ShapeDtypeStruct((B, N, 512), F32),
                   jax.ShapeDtypeStruct((B,) + st_shape, F32)],
        scratch_shapes=[pltpu.VMEM(st_shape, F32)],
        compiler_params=_params(("parallel", "arbitrary")),
        name="gla",
    )(kq, v, g2, kq, v, g2, s0)


def _merge_kernel(x_ref, g1_ref, a_ref, b_ref, of_ref, ob_ref, r_ref, gates_ref,
                  gn_ref, wbr_ref, wo_ref, o_ref):
    o = of_ref[0] + ob_ref[0]
    cs = []
    for h in range(GLA_HEADS):
        lanes = slice(h * LANES, (h + 1) * LANES)
        oh = o[:, lanes]
        cs.append(oh * lax.rsqrt(jnp.mean(oh * oh, axis=-1, keepdims=True) + EPS) * gn_ref[...])
    r = r_ref[0].astype(F32)
    c_out = (jnp.concatenate(cs, axis=1) * (r * jax.nn.sigmoid(r))).astype(BF16)
    branches = (a_ref[0], b_ref[0], c_out)
    mixed = None
    for k in range(N_BRANCH):
        gate = gates_ref[0, :, k * D_MODEL:(k + 1) * D_MODEL].astype(F32)
        term = gate * _dot(branches[k], wbr_ref[k])
        mixed = term if mixed is None else mixed + term
    y = _dot(mixed.astype(BF16), wo_ref[...])
    o_ref[0] = x_ref[0] + g1_ref[0] * y


def _merge(x, g1, a, b, o_f, o_b, r, gates, lw, tm):
    B, N, _ = x.shape
    tok = lambda w: pl.BlockSpec((1, tm, w), lambda b_, i: (b_, i, 0))
    mod = pl.BlockSpec((1, 1, D_MODEL), lambda b_, i: (b_, 0, 0))
    consts = [lw["gla_norm"], lw["w_br"], lw["w_o"]]
    return pl.pallas_call(
        _merge_kernel,
        grid=(B, N // tm),
        in_specs=[tok(D_MODEL), mod, tok(512), tok(512), tok(512), tok(512), tok(512), tok(3072)]
                 + [_const_spec(c.shape) for c in consts],
        out_specs=tok(D_MODEL),
        out_shape=jax.ShapeDtypeStruct((B, N, D_MODEL), F32),
        compiler_params=_params(("parallel", "parallel")),
        name="merge",
    )(x, g1, a, b, o_f, o_b, r, gates, *consts)


def _ffn_kernel(x_ref, sh_ref, sc_ref, g2_ref, w1_ref, w2_ref, o_ref):
    x = x_ref[0]
    hb = _modulated(x, sh_ref[0], sc_ref[0]).astype(BF16)
    t = jnp.maximum(_dot(hb, w1_ref[...]), 0.0)
    f = _dot((t * t).astype(BF16), w2_ref[...])
    o_ref[0] = x + g2_ref[0] * f


def _ffn(x, shift, scale, gate, lw, tm):
    B, N, _ = x.shape
    tok = pl.BlockSpec((1, tm, D_MODEL), lambda b, i: (b, i, 0))
    mod = pl.BlockSpec((1, 1, D_MODEL), lambda b, i: (b, 0, 0))
    return pl.pallas_call(
        _ffn_kernel,
        grid=(B, N // tm),
        in_specs=[tok, mod, mod, mod, _const_spec(lw["w_ff1"].shape), _const_spec(lw["w_ff2"].shape)],
        out_specs=tok,
        out_shape=jax.ShapeDtypeStruct((B, N, D_MODEL), F32),
        compiler_params=_params(("parallel", "parallel")),
        name="ffn",
    )(x, shift, scale, gate, lw["w_ff1"], lw["w_ff2"])


def _rope_tables(n_tokens):
    lane = np.arange(LANES)
    d = lane % HEAD_DIM
    use_col = (d // (HEAD_DIM // 2)) == 1
    second = ((d % (HEAD_DIM // 2)) // (HEAD_DIM // 4)) == 1
    f = d % (HEAD_DIM // 4)
    n_freq = HEAD_DIM // 4
    freqs = ROPE_BASE ** (-jnp.arange(n_freq, dtype=F32) / n_freq)
    t = jnp.arange(n_tokens)
    row = (t // GRID_W).astype(F32)
    col = (t % GRID_W).astype(F32)
    ang_row = row[:, None] * freqs[None, :]
    ang_col = col[:, None] * freqs[None, :]
    ang = jnp.where(jnp.asarray(use_col)[None, :], ang_col[:, f], ang_row[:, f])
    cos, sin = jnp.cos(ang), jnp.sin(ang)
    second = jnp.asarray(second)[None, :]
    return cos, jnp.where(second, sin, 0.0), jnp.where(second, 0.0, -sin)


def _identity_tables(n_tokens):
    return (jnp.ones((n_tokens, LANES), F32), jnp.zeros((n_tokens, LANES), F32),
            jnp.zeros((n_tokens, LANES), F32))


def _layer_weights(l, w_in, q_norm, k_norm, sgu_norm, w_sgu, b_sgu, w_gate_f, b_gate_f,
                   w_gate_b, b_gate_b, gla_norm, w_br, w_o, w_ff1, w_ff2):
    w = w_in[l]
    col = lambda i: w[:, _OFFS[i]:_OFFS[i + 1]]
    wq = col(2).reshape(D_MODEL, SWA_KV_HEADS, SWA_GROUP, HEAD_DIM).transpose(0, 2, 1, 3).reshape(D_MODEL, 512)
    wkq = jnp.stack([col(6).reshape(D_MODEL, GLA_HEADS, GLA_DK), col(5).reshape(D_MODEL, GLA_HEADS, GLA_DK)],
                    axis=2).reshape(D_MODEL, 512)
    w_z = jnp.zeros((D_MODEL, LANES), F32).at[:, :2 * GLA_RANK].set(jnp.concatenate([col(9), col(10)], axis=1))

    def dup(t):
        t = t.reshape(t.shape[:-1] + (GLA_HEADS, 1, GLA_DK))
        return jnp.broadcast_to(t, t.shape[:-2] + (2, GLA_DK)).reshape(t.shape[:-3] + (GLA_HEADS * LANES,))

    w2 = jnp.zeros((LANES, 2 * GLA_HEADS * LANES), F32)
    w2 = w2.at[:GLA_RANK, :512].set(dup(w_gate_f[l])).at[GLA_RANK:2 * GLA_RANK, 512:].set(dup(w_gate_b[l]))
    b2 = jnp.concatenate([dup(b_gate_f[l]), dup(b_gate_b[l])])[None, :]
    wbr1 = w_br[l, 1].reshape(SWA_KV_HEADS, SWA_GROUP, HEAD_DIM, D_MODEL).transpose(1, 0, 2, 3).reshape(512, D_MODEL)
    kq_scale = jnp.tile(jnp.concatenate([jnp.ones((GLA_DK,), F32), jnp.full((GLA_DK,), GLA_DK ** -0.5, F32)]),
                        GLA_HEADS)[None, :]
    return {
        "w_uv": jnp.concatenate([col(0), col(1)], axis=1).astype(BF16),
        "w_b": jnp.concatenate([wq, col(3), col(4)], axis=1).astype(BF16),
        "w_c": jnp.concatenate([wkq, col(7), col(8)], axis=1).astype(BF16),
        "w_z": w_z.astype(BF16),
        "w_g": col(11).astype(BF16),
        "w2": w2.astype(BF16),
        "b2": b2,
        "sgu_norm": sgu_norm[l][None, :],
        "w_sgu": w_sgu[l].astype(BF16),
        "b_sgu": jnp.broadcast_to(b_sgu[l][:, :, None], (SGU_GROUPS, SGU_CHUNK, LANES)),
        "q_norm": jnp.tile(q_norm[l], 2)[None, :],
        "k_norm": jnp.tile(k_norm[l], 2)[None, :],
        "kq_scale": kq_scale,
        "gla_norm": gla_norm[l][None, :],
        "w_br": jnp.stack([w_br[l, 0], wbr1, w_br[l, 2]]).astype(BF16),
        "w_o": w_o[l].astype(BF16),
        "w_ff1": w_ff1[l].astype(BF16),
        "w_ff2": w_ff2[l].astype(BF16),
    }


def kernel(x, c, ctx, c_ctx, w_ada, b_ada, w_in, q_norm, k_norm, sink, sgu_norm, w_sgu, b_sgu,
           w_gate_f, b_gate_f, w_gate_b, b_gate_b, gla_norm, w_br, w_o, w_ff1, w_ff2):
    B, N, _ = x.shape
    C = ctx.shape[1]
    depth = w_ada.shape[0]
    tm_lat, tm_ctx = 256, C
    gla_tile = 256

    cond = jnp.zeros((16, D_MODEL), F32).at[:B].set(c).at[B].set(c_ctx)
    mods = _adaln(cond, w_ada, b_ada)
    rope_lat = _rope_tables(N)
    rope_ctx = _identity_tables(C)
    zero_state = jnp.zeros((B, 2, GLA_HEADS, GLA_DV, LANES), F32)

    for l in range(depth):
        lw = _layer_weights(l, w_in, q_norm, k_norm, sgu_norm, w_sgu, b_sgu, w_gate_f, b_gate_f,
                            w_gate_b, b_gate_b, gla_norm, w_br, w_o, w_ff1, w_ff2)
        m = mods[l].reshape(16, 6, D_MODEL)
        lat = [m[:B, j][:, None, :] for j in range(6)]
        cm = [jnp.broadcast_to(m[B, j][None, None, :], (B, 1, D_MODEL)) for j in range(6)]

        (ca, cq, ck, cv, ckq, cvc, cr, cg2, cgates) = _proj(ctx, cm[0], cm[1], rope_ctx, lw, tm_ctx)
        oc_f, oc_b, s_ctx = _gla(ckq, cvc, cg2, zero_state, min(gla_tile, C))

        (a, q, k, v, kq, vc, r, g2, gates) = _proj(x, lat[0], lat[1], rope_lat, lw, tm_lat)
        b_out = _attn(q, k, v, ck, cv, sink[l], local=True)
        o_f, o_b, _ = _gla(kq, vc, g2, s_ctx, gla_tile)
        x_new = _merge(x, lat[2], a, b_out, o_f, o_b, r, gates, lw, tm_lat)
        x_new = _ffn(x_new, lat[3], lat[4], lat[5], lw, tm_lat)

        if l < depth - 1:
            cb_out = _attn(cq, None, None, ck, cv, sink[l], local=False)
            ctx_new = _merge(ctx, cm[2], ca, cb_out, oc_f, oc_b, cr, cgates, lw, tm_ctx)
            ctx = _ffn(ctx_new, cm[3], cm[4], cm[5], lw, tm_ctx)
        x = x_new

    return x
```

```python
import functools

import jax
import jax.numpy as jnp
import numpy as np
from jax import lax
from jax.experimental import pallas as pl
from jax.experimental.pallas import tpu as pltpu

D_MODEL = 1024
GRID_W = 64
HEAD_DIM = 64
BRANCH_WIDTH = 512
N_BRANCH = 3
SGU_GROUPS = 4
SGU_CHUNK = 128
SWA_Q_HEADS = 8
SWA_KV_HEADS = 2
SWA_GROUP = 4
SWA_WINDOW = 128
SWA_BLOCK = 128
ROPE_BASE = 10000.0
GLA_HEADS = 4
GLA_DK = 64
GLA_DV = 128
GLA_RANK = 16
GLA_NORMALIZER = 16.0
GLA_CHUNK = 64
FFN_DIM = 4 * D_MODEL
EPS = 1e-6

LANES = 128
VMEM_LIMIT = 56 * 1024 * 1024
NEG_BIG = -1e30
PROJ_SUBTILE = 256
BF16 = jnp.bfloat16
F32 = jnp.float32

_SPLITS = (512, 512, 512, 128, 128, 256, 256, 512, 512, 16, 16, 3072)
_OFFS = np.concatenate([[0], np.cumsum(_SPLITS)]).tolist()


def _dot(a, b):
    return jnp.dot(a, b, preferred_element_type=F32)


def _dot_nt(a, b):
    return lax.dot_general(a, b, (((1,), (1,)), ((), ())), preferred_element_type=F32)


def _dot_tn(a, b):
    return lax.dot_general(a, b, (((0,), (0,)), ((), ())), preferred_element_type=F32)


def _split_dot(a, m):
    hi = a.astype(BF16)
    lo = (a - hi.astype(F32)).astype(BF16)
    return _dot(hi, m) + _dot(lo, m)


def _const_spec(shape):
    nd = len(shape)
    return pl.BlockSpec(shape, lambda *_: (0,) * nd, pipeline_mode=pl.Buffered(1))


def _params(sem):
    return pltpu.CompilerParams(dimension_semantics=sem, vmem_limit_bytes=VMEM_LIMIT)


def _adaln_kernel(cond_ref, w_ref, b_ref, o_ref):
    cond = cond_ref[...]
    s = (cond * jax.nn.sigmoid(cond)).astype(BF16)
    o_ref[0] = _dot(s, w_ref[0].astype(BF16)) + b_ref[0]


def _adaln(cond, w_ada, b_ada):
    depth = w_ada.shape[0]
    rows = cond.shape[0]
    tn = 1536
    return pl.pallas_call(
        _adaln_kernel,
        grid=(depth, 6 * D_MODEL // tn),
        in_specs=[
            pl.BlockSpec((rows, D_MODEL), lambda l, j: (0, 0)),
            pl.BlockSpec((1, D_MODEL, tn), lambda l, j: (l, 0, j)),
            pl.BlockSpec((1, 1, tn), lambda l, j: (l, 0, j)),
        ],
        out_specs=pl.BlockSpec((1, rows, tn), lambda l, j: (l, 0, j)),
        out_shape=jax.ShapeDtypeStruct((depth, rows, 6 * D_MODEL), F32),
        compiler_params=_params(("arbitrary", "arbitrary")),
        name="adaln",
    )(cond, w_ada, b_ada.reshape(depth, 1, 6 * D_MODEL))


def _modulated(x, shift, scale):
    ms = jnp.mean(x * x, axis=-1, keepdims=True)
    return x * lax.rsqrt(ms + EPS) * (1.0 + scale) + shift


def _proj_kernel(x_ref, sh_ref, sc_ref, cos_ref, sa_ref, sb_ref,
                 wuv_ref, wb_ref, wc_ref, wz_ref, wg_ref, w2_ref, b2_ref,
                 sgun_ref, ws_ref, bs_ref, qn_ref, kn_ref, kqs_ref,
                 a_ref, q_ref, k_ref, v_ref, kq_ref, vc_ref, r_ref, g2_ref, gates_ref):
    ri = lax.broadcasted_iota(jnp.int32, (LANES, LANES), 0) // HEAD_DIM
    ci = lax.broadcasted_iota(jnp.int32, (LANES, LANES), 1) // HEAD_DIM
    group_ones = jnp.where(ri == ci, 1.0, 0.0).astype(BF16)

    for r0 in range(0, x_ref.shape[1], PROJ_SUBTILE):
        rows = slice(r0, r0 + PROJ_SUBTILE)
        hb = _modulated(x_ref[0, rows], sh_ref[0], sc_ref[0]).astype(BF16)
        cos, sa, sb = cos_ref[rows], sa_ref[rows], sb_ref[rows]

        def norm_rope(xs, gain):
            ss = _split_dot(xs * xs, group_ones)
            y = xs * lax.rsqrt(ss * (1.0 / HEAD_DIM) + EPS) * gain
            return y * cos + pltpu.roll(y, 16, 1) * sa + pltpu.roll(y, LANES - 16, 1) * sb

        uv = _dot(hb, wuv_ref[...])
        gates = _dot(hb, wg_ref[...])
        bq = _dot(hb, wb_ref[...])
        z = _dot(hb, wz_ref[...]).astype(BF16)
        cp = _dot(hb, wc_ref[...])

        gates_ref[0, rows] = jax.nn.sigmoid(gates).astype(gates_ref.dtype)

        for s in range(BRANCH_WIDTH // LANES):
            lanes = slice(s * LANES, (s + 1) * LANES)
            q_ref[0, rows, lanes] = (norm_rope(bq[:, lanes], qn_ref[...]) * HEAD_DIM ** -0.5).astype(q_ref.dtype)
        k_ref[0, rows] = norm_rope(bq[:, 512:640], kn_ref[...]).astype(k_ref.dtype)
        v_ref[0, rows] = bq[:, 640:768].astype(v_ref.dtype)

        logit = _dot(z, w2_ref[...]) + b2_ref[...]
        log_sig = jnp.minimum(logit, 0.0) - jnp.log1p(jnp.exp(-jnp.abs(logit)))
        g2_ref[0, rows] = log_sig * (1.0 / GLA_NORMALIZER)
        kq_ref[0, rows] = cp[:, :512] * kqs_ref[...]
        vc_ref[0, rows] = cp[:, 512:1024].astype(vc_ref.dtype)
        r_ref[0, rows] = cp[:, 1024:1536].astype(r_ref.dtype)

        u = jax.nn.gelu(uv[:, :BRANCH_WIDTH])
        va = jax.nn.gelu(uv[:, BRANCH_WIDTH:])
        for g in range(SGU_GROUPS):
            lanes = slice(g * LANES, (g + 1) * LANES)
            vg = va[:, lanes]
            vg = vg * lax.rsqrt(jnp.mean(vg * vg, axis=-1, keepdims=True) + EPS) * sgun_ref[:, lanes]
            vgb = vg.astype(BF16)
            for c in range(PROJ_SUBTILE // SGU_CHUNK):
                cr = slice(c * SGU_CHUNK, (c + 1) * SGU_CHUNK)
                s = _dot(ws_ref[g], vgb[cr]) + bs_ref[g]
                a_ref[0, r0 + c * SGU_CHUNK:r0 + (c + 1) * SGU_CHUNK, lanes] = (u[cr, lanes] * s).astype(a_ref.dtype)


def _proj(x, shift, scale, rope, lw, tm):
    B, N, _ = x.shape
    cos, sa, sb = rope
    tok = lambda w: pl.BlockSpec((1, tm, w), lambda b, i: (b, i, 0))
    mod = pl.BlockSpec((1, 1, D_MODEL), lambda b, i: (b, 0, 0))
    tab = pl.BlockSpec((tm, LANES), lambda b, i: (i, 0))
    consts = [lw["w_uv"], lw["w_b"], lw["w_c"], lw["w_z"], lw["w_g"], lw["w2"], lw["b2"],
              lw["sgu_norm"], lw["w_sgu"], lw["b_sgu"], lw["q_norm"], lw["k_norm"], lw["kq_scale"]]
    out_w = [(512, BF16), (512, BF16), (128, BF16), (128, BF16), (512, F32), (512, BF16),
             (512, BF16), (1024, F32), (3072, BF16)]
    return pl.pallas_call(
        _proj_kernel,
        grid=(B, N // tm),
        in_specs=[tok(D_MODEL), mod, mod, tab, tab, tab] + [_const_spec(c.shape) for c in consts],
        out_specs=[tok(w) for w, _ in out_w],
        out_shape=[jax.ShapeDtypeStruct((B, N, w), dt) for w, dt in out_w],
        compiler_params=_params(("parallel", "parallel")),
        name="proj",
    )(x, shift, scale, cos, sa, sb, *consts)


def _attn_kernel(sink_ref, q_ref, *refs, local):
    if local:
        band_ref, kp_ref, kc_ref, kn_ref, vp_ref, vc_ref, vn_ref, kx_ref, vx_ref, o_ref = refs
        keys = jnp.concatenate([kp_ref[0], kc_ref[0], kn_ref[0], kx_ref[0]], axis=0)
        vals = jnp.concatenate([vp_ref[0], vc_ref[0], vn_ref[0], vx_ref[0]], axis=0)
        blk = q_ref.shape[1]
        i = pl.program_id(1)
        col = lax.broadcasted_iota(jnp.int32, (1, keys.shape[0]), 1)
        no_prev = jnp.where(i == 0, NEG_BIG, 0.0)
        no_next = jnp.where(i == pl.num_programs(1) - 1, NEG_BIG, 0.0)
        edge = jnp.where(col < blk, no_prev, jnp.where((col >= 2 * blk) & (col < 3 * blk), no_next, 0.0))
        bias = band_ref[...] + edge
    else:
        kx_ref, vx_ref, o_ref = refs
        keys, vals = kx_ref[0], vx_ref[0]
    blk = q_ref.shape[1]
    n_slab = BRANCH_WIDTH // LANES
    lane = lax.broadcasted_iota(jnp.int32, (blk, LANES), 1)
    head0_lanes = lane < HEAD_DIM
    scores = {}
    for s in range(n_slab):
        q = q_ref[0, :, s * LANES:(s + 1) * LANES]
        for h in range(SWA_KV_HEADS):
            qh = jnp.where(head0_lanes if h == 0 else ~head0_lanes, q, jnp.zeros_like(q))
            scores[s, h] = qh
    stacked = {h: _dot_nt(jnp.concatenate([scores[s, h] for s in range(n_slab)], axis=0), keys)
               for h in range(SWA_KV_HEADS)}
    outs = {}
    for h in range(SWA_KV_HEADS):
        for s in range(n_slab):
            sc = stacked[h][s * blk:(s + 1) * blk]
            if local:
                sc = jnp.concatenate([sc[:, :blk] + bias[:, :blk], sc[:, blk:2 * blk],
                                      sc[:, 2 * blk:3 * blk] + bias[:, 2 * blk:3 * blk], sc[:, 3 * blk:]], axis=1)
            sink = sink_ref[h * SWA_GROUP + s]
            m = jnp.maximum(jnp.max(sc, axis=-1, keepdims=True), sink)
            p = jnp.exp(sc - m)
            denom = jnp.sum(p, axis=-1, keepdims=True) + jnp.exp(sink - m)
            outs[s, h] = (p.astype(BF16), 1.0 / denom)
        pv = _dot(jnp.concatenate([outs[s, h][0] for s in range(n_slab)], axis=0), vals)
        for s in range(n_slab):
            outs[s, h] = pv[s * blk:(s + 1) * blk] * outs[s, h][1]
    for s in range(n_slab):
        o_ref[0, :, s * LANES:(s + 1) * LANES] = jnp.where(head0_lanes, outs[s, 0], outs[s, 1]).astype(o_ref.dtype)


def _band_bias(blk, n_ctx):
    qi = np.arange(blk)[:, None]
    kj = np.arange(3 * blk + n_ctx)[None, :]
    ok = (np.abs(kj - blk - qi) <= SWA_WINDOW) | (kj >= 3 * blk)
    return jnp.asarray(np.where(ok, 0.0, NEG_BIG), F32)


def _attn(q, k, v, k_ctx, v_ctx, sink, local):
    B, N, _ = q.shape
    C = k_ctx.shape[1]
    blk = SWA_BLOCK
    nb = N // blk
    qspec = pl.BlockSpec((1, blk, BRANCH_WIDTH), lambda b, i: (b, i, 0))
    ctx_spec = pl.BlockSpec((1, C, LANES), lambda b, i: (b, 0, 0))
    smem = pl.BlockSpec(memory_space=pltpu.SMEM)
    if local:
        prev = pl.BlockSpec((1, blk, LANES), lambda b, i: (b, jnp.maximum(i - 1, 0), 0))
        cur = pl.BlockSpec((1, blk, LANES), lambda b, i: (b, i, 0))
        nxt = pl.BlockSpec((1, blk, LANES), lambda b, i: (b, jnp.minimum(i + 1, nb - 1), 0))
        band = _band_bias(blk, C)
        in_specs = [smem, qspec, _const_spec(band.shape), prev, cur, nxt, prev, cur, nxt, ctx_spec, ctx_spec]
        args = (sink, q, band, k, k, k, v, v, v, k_ctx, v_ctx)
    else:
        in_specs = [smem, qspec, ctx_spec, ctx_spec]
        args = (sink, q, k_ctx, v_ctx)
    return pl.pallas_call(
        functools.partial(_attn_kernel, local=local),
        grid=(B, nb),
        in_specs=in_specs,
        out_specs=qspec,
        out_shape=jax.ShapeDtypeStruct((B, N, BRANCH_WIDTH), BF16),
        compiler_params=_params(("parallel", "parallel")),
        name="attn_local" if local else "attn_ctx",
    )(*args)


def _gla_kernel(kqf_ref, vf_ref, gf_ref, kqb_ref, vb_ref, gb_ref, s0_ref,
                of_ref, ob_ref, sfin_ref, st_ref):
    i = pl.program_id(1)
    T = kqf_ref.shape[1]
    n_chunk = T // GLA_CHUNK

    @pl.when(i == 0)
    def _():
        st_ref[...] = s0_ref[0]

    ri = lax.broadcasted_iota(jnp.int32, (T, T), 0)
    ci = lax.broadcasted_iota(jnp.int32, (T, T), 1)
    same_chunk = (ri // GLA_CHUNK) == (ci // GLA_CHUNK)
    li = lax.broadcasted_iota(jnp.int32, (GLA_CHUNK, GLA_CHUNK), 0)
    mi = lax.broadcasted_iota(jnp.int32, (GLA_CHUNK, GLA_CHUNK), 1)
    lane = lax.broadcasted_iota(jnp.int32, (GLA_CHUNK, LANES), 1)
    k_lanes = lane < GLA_DK

    dirs = ((kqf_ref, vf_ref, gf_ref, of_ref), (kqb_ref, vb_ref, gb_ref, ob_ref))
    ops = {}
    for d, (kq_ref, v_ref, g_ref, o_ref) in enumerate(dirs):
        backward = d == 1
        tri_full = jnp.where(same_chunk & ((ci >= ri) if backward else (ci <= ri)), 1.0, 0.0).astype(BF16)
        cum = _split_dot_left(tri_full, g_ref[0])
        ref_row = GLA_CHUNK // 2 - 1 if backward else GLA_CHUNK // 2
        tot_row = 0 if backward else GLA_CHUNK - 1
        for h in range(GLA_HEADS):
            lanes = slice(h * LANES, (h + 1) * LANES)
            for c in range(n_chunk):
                rows = slice(c * GLA_CHUNK, (c + 1) * GLA_CHUNK)
                bc = cum[rows, lanes]
                x = kq_ref[0, rows, lanes]
                b_ref = bc[ref_row:ref_row + 1]
                b_tot = bc[tot_row:tot_row + 1]
                xe1 = x * jnp.exp(jnp.where(k_lanes, b_ref - bc, bc - b_ref))
                xe2 = x * jnp.exp(jnp.where(k_lanes, b_tot - bc, bc))
                ops[d, h, c] = dict(
                    ke=xe1.astype(BF16),
                    qe=jnp.where(k_lanes, pltpu.roll(xe1, GLA_DK, 1), 0.0).astype(BF16),
                    q_in=pltpu.roll(xe2, GLA_DK, 1).astype(BF16),
                    k_out=jnp.where(k_lanes, xe2, 0.0).astype(BF16),
                    decay=jnp.exp(b_tot))

    for (d, h, c), u in ops.items():
        keep = (mi >= li) if d == 1 else (mi <= li)
        u["pm"] = jnp.where(keep, _dot_nt(u["qe"], u["ke"]), 0.0).astype(BF16)

    for step in range(n_chunk):
        for d, (kq_ref, v_ref, g_ref, o_ref) in enumerate(dirs):
            c = n_chunk - 1 - step if d == 1 else step
            rows = slice(c * GLA_CHUNK, (c + 1) * GLA_CHUNK)
            for h in range(GLA_HEADS):
                lanes = slice(h * LANES, (h + 1) * LANES)
                u = ops[d, h, c]
                vch = v_ref[0, rows, lanes]
                st = st_ref[d, h]
                o_ref[0, rows, lanes] = _dot(u["pm"], vch) + _dot_nt(u["q_in"], st.astype(BF16))
                st_ref[d, h] = st * u["decay"] + _dot_tn(vch, u["k_out"])

    @pl.when(i == pl.num_programs(1) - 1)
    def _():
        sfin_ref[0] = st_ref[...]


def _split_dot_left(m, a):
    hi = a.astype(BF16)
    lo = (a - hi.astype(F32)).astype(BF16)
    return _dot(m, hi) + _dot(m, lo)


def _gla(kq, v, g2, s0, T):
    B, N, _ = kq.shape
    nt = N // T
    fwd = lambda w: pl.BlockSpec((1, T, w), lambda b, i: (b, i, 0))
    bwd = lambda w: pl.BlockSpec((1, T, w), lambda b, i: (b, nt - 1 - i, 0))
    g_fwd = pl.BlockSpec((1, T, 512), lambda b, i: (b, i, 0))
    g_bwd = pl.BlockSpec((1, T, 512), lambda b, i: (b, nt - 1 - i, 1))
    st_shape = (2, GLA_HEADS, GLA_DV, LANES)
    st_spec = pl.BlockSpec((1,) + st_shape, lambda b, i: (b, 0, 0, 0, 0))
    return pl.pallas_call(
        _gla_kernel,
        grid=(B, nt),
        in_specs=[fwd(512), fwd(512), g_fwd, bwd(512), bwd(512), g_bwd, st_spec],
        out_specs=[fwd(512), bwd(512), st_spec],
        out_shape=[jax.ShapeDtypeStruct((B, N, 512), F32), jax.ShapeDtypeStruct((B, N, 512), F32),
                   jax.ShapeDtypeStruct((B,) + st_shape, F32)],
        scratch_shapes=[pltpu.VMEM(st_shape, F32)],
        compiler_params=_params(("parallel", "arbitrary")),
        name="gla",
    )(kq, v, g2, kq, v, g2, s0)


def _merge_kernel(x_ref, g1_ref, a_ref, b_ref, of_ref, ob_ref, r_ref, gates_ref,
                  gn_ref, wbr_ref, wo_ref, o_ref):
    o = of_ref[0] + ob_ref[0]
    cs = []
    for h in range(GLA_HEADS):
        lanes = slice(h * LANES, (h + 1) * LANES)
        oh = o[:, lanes]
        cs.append(oh * lax.rsqrt(jnp.mean(oh * oh, axis=-1, keepdims=True) + EPS) * gn_ref[...])
    r = r_ref[0].astype(F32)
    c_out = (jnp.concatenate(cs, axis=1) * (r * jax.nn.sigmoid(r))).astype(BF16)
    branches = (a_ref[0], b_ref[0], c_out)
    mixed = None
    for k in range(N_BRANCH):
        gate = gates_ref[0, :, k * D_MODEL:(k + 1) * D_MODEL].astype(F32)
        term = gate * _dot(branches[k], wbr_ref[k])
        mixed = term if mixed is None else mixed + term
    y = _dot(mixed.astype(BF16), wo_ref[...])
    o_ref[0] = x_ref[0] + g1_ref[0] * y


def _merge(x, g1, a, b, o_f, o_b, r, gates, lw, tm):
    B, N, _ = x.shape
    tok = lambda w: pl.BlockSpec((1, tm, w), lambda b_, i: (b_, i, 0))
    mod = pl.BlockSpec((1, 1, D_MODEL), lambda b_, i: (b_, 0, 0))
    consts = [lw["gla_norm"], lw["w_br"], lw["w_o"]]
    return pl.pallas_call(
        _merge_kernel,
        grid=(B, N // tm),
        in_specs=[tok(D_MODEL), mod, tok(512), tok(512), tok(512), tok(512), tok(512), tok(3072)]
                 + [_const_spec(c.shape) for c in consts],
        out_specs=tok(D_MODEL),
        out_shape=jax.ShapeDtypeStruct((B, N, D_MODEL), F32),
        compiler_params=_params(("parallel", "parallel")),
        name="merge",
    )(x, g1, a, b, o_f, o_b, r, gates, *consts)


def _ffn_kernel(x_ref, sh_ref, sc_ref, g2_ref, w1_ref, w2_ref, o_ref):
    x = x_ref[0]
    hb = _modulated(x, sh_ref[0], sc_ref[0]).astype(BF16)
    t = jnp.maximum(_dot(hb, w1_ref[...]), 0.0)
    f = _dot((t * t).astype(BF16), w2_ref[...])
    o_ref[0] = x + g2_ref[0] * f


def _ffn(x, shift, scale, gate, lw, tm):
    B, N, _ = x.shape
    tok = pl.BlockSpec((1, tm, D_MODEL), lambda b, i: (b, i, 0))
    mod = pl.BlockSpec((1, 1, D_MODEL), lambda b, i: (b, 0, 0))
    return pl.pallas_call(
        _ffn_kernel,
        grid=(B, N // tm),
        in_specs=[tok, mod, mod, mod, _const_spec(lw["w_ff1"].shape), _const_spec(lw["w_ff2"].shape)],
        out_specs=tok,
        out_shape=jax.ShapeDtypeStruct((B, N, D_MODEL), F32),
        compiler_params=_params(("parallel", "parallel")),
        name="ffn",
    )(x, shift, scale, gate, lw["w_ff1"], lw["w_ff2"])


def _rope_tables(n_tokens):
    lane = np.arange(LANES)
    d = lane % HEAD_DIM
    use_col = (d // (HEAD_DIM // 2)) == 1
    second = ((d % (HEAD_DIM // 2)) // (HEAD_DIM // 4)) == 1
    f = d % (HEAD_DIM // 4)
    n_freq = HEAD_DIM // 4
    freqs = ROPE_BASE ** (-jnp.arange(n_freq, dtype=F32) / n_freq)
    t = jnp.arange(n_tokens)
    row = (t // GRID_W).astype(F32)
    col = (t % GRID_W).astype(F32)
    ang_row = row[:, None] * freqs[None, :]
    ang_col = col[:, None] * freqs[None, :]
    ang = jnp.where(jnp.asarray(use_col)[None, :], ang_col[:, f], ang_row[:, f])
    cos, sin = jnp.cos(ang), jnp.sin(ang)
    second = jnp.asarray(second)[None, :]
    return cos, jnp.where(second, sin, 0.0), jnp.where(second, 0.0, -sin)


def _identity_tables(n_tokens):
    return (jnp.ones((n_tokens, LANES), F32), jnp.zeros((n_tokens, LANES), F32),
            jnp.zeros((n_tokens, LANES), F32))


def _layer_weights(l, w_in, q_norm, k_norm, sgu_norm, w_sgu, b_sgu, w_gate_f, b_gate_f,
                   w_gate_b, b_gate_b, gla_norm, w_br, w_o, w_ff1, w_ff2):
    w = w_in[l]
    col = lambda i: w[:, _OFFS[i]:_OFFS[i + 1]]
    wq = col(2).reshape(D_MODEL, SWA_KV_HEADS, SWA_GROUP, HEAD_DIM).transpose(0, 2, 1, 3).reshape(D_MODEL, 512)
    wkq = jnp.stack([col(6).reshape(D_MODEL, GLA_HEADS, GLA_DK), col(5).reshape(D_MODEL, GLA_HEADS, GLA_DK)],
                    axis=2).reshape(D_MODEL, 512)
    w_z = jnp.zeros((D_MODEL, LANES), F32).at[:, :2 * GLA_RANK].set(jnp.concatenate([col(9), col(10)], axis=1))

    def dup(t):
        t = t.reshape(t.shape[:-1] + (GLA_HEADS, 1, GLA_DK))
        return jnp.broadcast_to(t, t.shape[:-2] + (2, GLA_DK)).reshape(t.shape[:-3] + (GLA_HEADS * LANES,))

    w2 = jnp.zeros((LANES, 2 * GLA_HEADS * LANES), F32)
    w2 = w2.at[:GLA_RANK, :512].set(dup(w_gate_f[l])).at[GLA_RANK:2 * GLA_RANK, 512:].set(dup(w_gate_b[l]))
    b2 = jnp.concatenate([dup(b_gate_f[l]), dup(b_gate_b[l])])[None, :]
    wbr1 = w_br[l, 1].reshape(SWA_KV_HEADS, SWA_GROUP, HEAD_DIM, D_MODEL).transpose(1, 0, 2, 3).reshape(512, D_MODEL)
    kq_scale = jnp.tile(jnp.concatenate([jnp.ones((GLA_DK,), F32), jnp.full((GLA_DK,), GLA_DK ** -0.5, F32)]),
                        GLA_HEADS)[None, :]
    return {
        "w_uv": jnp.concatenate([col(0), col(1)], axis=1).astype(BF16),
        "w_b": jnp.concatenate([wq, col(3), col(4)], axis=1).astype(BF16),
        "w_c": jnp.concatenate([wkq, col(7), col(8)], axis=1).astype(BF16),
        "w_z": w_z.astype(BF16),
        "w_g": col(11).astype(BF16),
        "w2": w2.astype(BF16),
        "b2": b2,
        "sgu_norm": sgu_norm[l][None, :],
        "w_sgu": w_sgu[l].astype(BF16),
        "b_sgu": jnp.broadcast_to(b_sgu[l][:, :, None], (SGU_GROUPS, SGU_CHUNK, LANES)),
        "q_norm": jnp.tile(q_norm[l], 2)[None, :],
        "k_norm": jnp.tile(k_norm[l], 2)[None, :],
        "kq_scale": kq_scale,
        "gla_norm": gla_norm[l][None, :],
        "w_br": jnp.stack([w_br[l, 0], wbr1, w_br[l, 2]]).astype(BF16),
        "w_o": w_o[l].astype(BF16),
        "w_ff1": w_ff1[l].astype(BF16),
        "w_ff2": w_ff2[l].astype(BF16),
    }


def kernel(x, c, ctx, c_ctx, w_ada, b_ada, w_in, q_norm, k_norm, sink, sgu_norm, w_sgu, b_sgu,
           w_gate_f, b_gate_f, w_gate_b, b_gate_b, gla_norm, w_br, w_o, w_ff1, w_ff2):
    B, N, _ = x.shape
    C = ctx.shape[1]
    depth = w_ada.shape[0]
    tm_lat, tm_ctx = 512, C
    gla_tile = 256

    cond = jnp.zeros((16, D_MODEL), F32).at[:B].set(c).at[B].set(c_ctx)
    mods = _adaln(cond, w_ada, b_ada)
    rope_lat = _rope_tables(N)
    rope_ctx = _identity_tables(C)
    zero_state = jnp.zeros((B, 2, GLA_HEADS, GLA_DV, LANES), F32)

    for l in range(depth):
        lw = _layer_weights(l, w_in, q_norm, k_norm, sgu_norm, w_sgu, b_sgu, w_gate_f, b_gate_f,
                            w_gate_b, b_gate_b, gla_norm, w_br, w_o, w_ff1, w_ff2)
        m = mods[l].reshape(16, 6, D_MODEL)
        lat = [m[:B, j][:, None, :] for j in range(6)]
        cm = [jnp.broadcast_to(m[B, j][None, None, :], (B, 1, D_MODEL)) for j in range(6)]

        (ca, cq, ck, cv, ckq, cvc, cr, cg2, cgates) = _proj(ctx, cm[0], cm[1], rope_ctx, lw, tm_ctx)
        oc_f, oc_b, s_ctx = _gla(ckq, cvc, cg2, zero_state, min(gla_tile, C))

        (a, q, k, v, kq, vc, r, g2, gates) = _proj(x, lat[0], lat[1], rope_lat, lw, tm_lat)
        b_out = _attn(q, k, v, ck, cv, sink[l], local=True)
        o_f, o_b, _ = _gla(kq, vc, g2, s_ctx, gla_tile)
        x_new = _merge(x, lat[2], a, b_out, o_f, o_b, r, gates, lw, tm_lat)
        x_new = _ffn(x_new, lat[3], lat[4], lat[5], lw, tm_lat)

        if l < depth - 1:
            cb_out = _attn(cq, None, None, ck, cv, sink[l], local=False)
            ctx_new = _merge(ctx, cm[2], ca, cb_out, oc_f, oc_b, cr, cgates, lw, tm_ctx)
            ctx = _ffn(ctx_new, cm[3], cm[4], cm[5], lw, tm_ctx)
        x = x_new

    return x
```

```python
import functools

import jax
import jax.numpy as jnp
import numpy as np
from jax import lax
from jax.experimental import pallas as pl
from jax.experimental.pallas import tpu as pltpu

D_MODEL = 1024
GRID_W = 64
HEAD_DIM = 64
BRANCH_WIDTH = 512
N_BRANCH = 3
SGU_GROUPS = 4
SGU_CHUNK = 128
SWA_Q_HEADS = 8
SWA_KV_HEADS = 2
SWA_GROUP = 4
SWA_WINDOW = 128
SWA_BLOCK = 128
ROPE_BASE = 10000.0
GLA_HEADS = 4
GLA_DK = 64
GLA_DV = 128
GLA_RANK = 16
GLA_NORMALIZER = 16.0
GLA_CHUNK = 64
FFN_DIM = 4 * D_MODEL
EPS = 1e-6

LANES = 128
VMEM_LIMIT = 56 * 1024 * 1024
NEG_BIG = -1e30
PROJ_SUBTILE = 256
GLA_CUM_ROWS = 256
BF16 = jnp.bfloat16
F32 = jnp.float32

_SPLITS = (512, 512, 512, 128, 128, 256, 256, 512, 512, 16, 16, 3072)
_OFFS = np.concatenate([[0], np.cumsum(_SPLITS)]).tolist()


def _dot(a, b):
    return jnp.dot(a, b, preferred_element_type=F32)


def _dot_nt(a, b):
    return lax.dot_general(a, b, (((1,), (1,)), ((), ())), preferred_element_type=F32)


def _dot_tn(a, b):
    return lax.dot_general(a, b, (((0,), (0,)), ((), ())), preferred_element_type=F32)


def _split_dot(a, m):
    hi = a.astype(BF16)
    lo = (a - hi.astype(F32)).astype(BF16)
    return _dot(hi, m) + _dot(lo, m)


def _const_spec(shape):
    nd = len(shape)
    return pl.BlockSpec(shape, lambda *_: (0,) * nd, pipeline_mode=pl.Buffered(1))


def _params(sem):
    return pltpu.CompilerParams(dimension_semantics=sem, vmem_limit_bytes=VMEM_LIMIT)


def _adaln_kernel(cond_ref, w_ref, b_ref, o_ref):
    cond = cond_ref[...]
    s = (cond * jax.nn.sigmoid(cond)).astype(BF16)
    o_ref[0] = _dot(s, w_ref[0].astype(BF16)) + b_ref[0]


def _adaln(cond, w_ada, b_ada):
    depth = w_ada.shape[0]
    rows = cond.shape[0]
    tn = 1536
    return pl.pallas_call(
        _adaln_kernel,
        grid=(depth, 6 * D_MODEL // tn),
        in_specs=[
            pl.BlockSpec((rows, D_MODEL), lambda l, j: (0, 0)),
            pl.BlockSpec((1, D_MODEL, tn), lambda l, j: (l, 0, j)),
            pl.BlockSpec((1, 1, tn), lambda l, j: (l, 0, j)),
        ],
        out_specs=pl.BlockSpec((1, rows, tn), lambda l, j: (l, 0, j)),
        out_shape=jax.ShapeDtypeStruct((depth, rows, 6 * D_MODEL), F32),
        compiler_params=_params(("arbitrary", "arbitrary")),
        name="adaln",
    )(cond, w_ada, b_ada.reshape(depth, 1, 6 * D_MODEL))


def _modulated(x, shift, scale):
    ms = jnp.mean(x * x, axis=-1, keepdims=True)
    return x * lax.rsqrt(ms + EPS) * (1.0 + scale) + shift


def _proj_kernel(x_ref, sh_ref, sc_ref, cos_ref, sa_ref, sb_ref,
                 wuv_ref, wb_ref, wc_ref, wz_ref, w2_ref, b2_ref,
                 sgun_ref, ws_ref, bs_ref, qn_ref, kn_ref, kqs_ref,
                 a_ref, q_ref, k_ref, v_ref, kq_ref, vc_ref, r_ref, g2_ref):
    ri = lax.broadcasted_iota(jnp.int32, (2 * LANES, 2 * LANES), 0) // HEAD_DIM
    ci = lax.broadcasted_iota(jnp.int32, (2 * LANES, 2 * LANES), 1) // HEAD_DIM
    group_ones = jnp.where(ri == ci, 1.0, 0.0).astype(BF16)

    tiles = [slice(r0, r0 + PROJ_SUBTILE) for r0 in range(0, x_ref.shape[1], PROJ_SUBTILE)]
    hbs = [_modulated(x_ref[0, rows], sh_ref[0], sc_ref[0]).astype(BF16) for rows in tiles]

    big = []
    for hb in hbs:
        uv = _dot(hb, wuv_ref[...])
        bq = _dot(hb, wb_ref[...])
        z = _dot(hb, wz_ref[...]).astype(BF16)
        cp = _dot(hb, wc_ref[...])
        big.append((uv, bq, z, cp))

    for rows, (uv, bq, z, cp) in zip(tiles, big):
        logit = _dot(z, w2_ref[...]) + b2_ref[...]
        log_sig = jnp.minimum(logit, 0.0) - jnp.log1p(jnp.exp(-jnp.abs(logit)))
        g2_ref[0, rows] = log_sig * (1.0 / GLA_NORMALIZER)
        kq_ref[0, rows] = cp[:, :512] * kqs_ref[...]
        vc_ref[0, rows] = cp[:, 512:1024].astype(vc_ref.dtype)
        r_ref[0, rows] = cp[:, 1024:1536].astype(r_ref.dtype)

    for rows, (uv, bq, z, cp) in zip(tiles, big):
        cos, sa, sb = cos_ref[rows], sa_ref[rows], sb_ref[rows]

        def head_norm(xs):
            ss = _split_dot(xs * xs, group_ones)
            return xs * lax.rsqrt(ss * (1.0 / HEAD_DIM) + EPS)

        def rope(y):
            return y * cos + pltpu.roll(y, 16, 1) * sa + pltpu.roll(y, LANES - 16, 1) * sb

        for j in range(BRANCH_WIDTH // (2 * LANES)):
            qn = head_norm(bq[:, j * 2 * LANES:(j + 1) * 2 * LANES])
            for s in range(2):
                lanes = slice((2 * j + s) * LANES, (2 * j + s + 1) * LANES)
                qs = rope(qn[:, s * LANES:(s + 1) * LANES] * qn_ref[...]) * HEAD_DIM ** -0.5
                q_ref[0, rows, lanes] = qs.astype(q_ref.dtype)
        kvn = head_norm(bq[:, 512:768])
        k_ref[0, rows] = rope(kvn[:, :LANES] * kn_ref[...]).astype(k_ref.dtype)
        v_ref[0, rows] = bq[:, 640:768].astype(v_ref.dtype)

    for rows, (uv, bq, z, cp) in zip(tiles, big):
        u = jax.nn.gelu(uv[:, :BRANCH_WIDTH])
        va = jax.nn.gelu(uv[:, BRANCH_WIDTH:])
        for g in range(SGU_GROUPS):
            lanes = slice(g * LANES, (g + 1) * LANES)
            vg = va[:, lanes]
            vg = vg * lax.rsqrt(jnp.mean(vg * vg, axis=-1, keepdims=True) + EPS) * sgun_ref[:, lanes]
            vgb = vg.astype(BF16)
            for c in range(PROJ_SUBTILE // SGU_CHUNK):
                cr = slice(c * SGU_CHUNK, (c + 1) * SGU_CHUNK)
                out_rows = slice(rows.start + c * SGU_CHUNK, rows.start + (c + 1) * SGU_CHUNK)
                s = _dot(ws_ref[g], vgb[cr]) + bs_ref[g]
                a_ref[0, out_rows, lanes] = (u[cr, lanes] * s).astype(a_ref.dtype)


def _proj(x, shift, scale, rope, lw, tm):
    B, N, _ = x.shape
    cos, sa, sb = rope
    tok = lambda w: pl.BlockSpec((1, tm, w), lambda b, i: (b, i, 0))
    mod = pl.BlockSpec((1, 1, D_MODEL), lambda b, i: (b, 0, 0))
    tab = pl.BlockSpec((tm, LANES), lambda b, i: (i, 0))
    consts = [lw["w_uv"], lw["w_b"], lw["w_c"], lw["w_z"], lw["w2"], lw["b2"],
              lw["sgu_norm"], lw["w_sgu"], lw["b_sgu"], lw["q_norm"], lw["k_norm"], lw["kq_scale"]]
    out_w = [(512, BF16), (512, BF16), (128, BF16), (128, BF16), (512, F32), (512, BF16),
             (512, BF16), (512, F32)]
    return pl.pallas_call(
        _proj_kernel,
        grid=(B, N // tm),
        in_specs=[tok(D_MODEL), mod, mod, tab, tab, tab] + [_const_spec(c.shape) for c in consts],
        out_specs=[tok(w) for w, _ in out_w],
        out_shape=[jax.ShapeDtypeStruct((B, N, w), dt) for w, dt in out_w],
        compiler_params=_params(("parallel", "parallel")),
        name="proj",
    )(x, shift, scale, cos, sa, sb, *consts)


def _attn_kernel(sink_ref, q_ref, *refs, local):
    if local:
        band_ref, kp_ref, kc_ref, kn_ref, vp_ref, vc_ref, vn_ref, kx_ref, vx_ref, o_ref = refs
        keys = jnp.concatenate([kp_ref[0], kc_ref[0], kn_ref[0], kx_ref[0]], axis=0)
        vals = jnp.concatenate([vp_ref[0], vc_ref[0], vn_ref[0], vx_ref[0]], axis=0)
        blk = q_ref.shape[1]
        i = pl.program_id(1)
        col = lax.broadcasted_iota(jnp.int32, (1, keys.shape[0]), 1)
        no_prev = jnp.where(i == 0, NEG_BIG, 0.0)
        no_next = jnp.where(i == pl.num_programs(1) - 1, NEG_BIG, 0.0)
        edge = jnp.where(col < blk, no_prev, jnp.where((col >= 2 * blk) & (col < 3 * blk), no_next, 0.0))
        bias = band_ref[...] + edge
    else:
        kx_ref, vx_ref, o_ref = refs
        keys, vals = kx_ref[0], vx_ref[0]
    blk = q_ref.shape[1]
    n_slab = BRANCH_WIDTH // LANES
    lane = lax.broadcasted_iota(jnp.int32, (blk, LANES), 1)
    head0_lanes = lane < HEAD_DIM
    scores = {}
    for s in range(n_slab):
        q = q_ref[0, :, s * LANES:(s + 1) * LANES]
        for h in range(SWA_KV_HEADS):
            qh = jnp.where(head0_lanes if h == 0 else ~head0_lanes, q, jnp.zeros_like(q))
            scores[s, h] = qh
    stacked = {h: _dot_nt(jnp.concatenate([scores[s, h] for s in range(n_slab)], axis=0), keys)
               for h in range(SWA_KV_HEADS)}
    outs = {}
    for h in range(SWA_KV_HEADS):
        for s in range(n_slab):
            sc = stacked[h][s * blk:(s + 1) * blk]
            if local:
                sc = jnp.concatenate([sc[:, :blk] + bias[:, :blk], sc[:, blk:2 * blk],
                                      sc[:, 2 * blk:3 * blk] + bias[:, 2 * blk:3 * blk], sc[:, 3 * blk:]], axis=1)
            sink = sink_ref[h * SWA_GROUP + s]
            m = jnp.maximum(jnp.max(sc, axis=-1, keepdims=True), sink)
            p = jnp.exp(sc - m)
            denom = jnp.sum(p, axis=-1, keepdims=True) + jnp.exp(sink - m)
            outs[s, h] = (p.astype(BF16), 1.0 / denom)
        pv = _dot(jnp.concatenate([outs[s, h][0] for s in range(n_slab)], axis=0), vals)
        for s in range(n_slab):
            outs[s, h] = pv[s * blk:(s + 1) * blk] * outs[s, h][1]
    for s in range(n_slab):
        o_ref[0, :, s * LANES:(s + 1) * LANES] = jnp.where(head0_lanes, outs[s, 0], outs[s, 1]).astype(o_ref.dtype)


def _band_bias(blk, n_ctx):
    qi = np.arange(blk)[:, None]
    kj = np.arange(3 * blk + n_ctx)[None, :]
    ok = (np.abs(kj - blk - qi) <= SWA_WINDOW) | (kj >= 3 * blk)
    return jnp.asarray(np.where(ok, 0.0, NEG_BIG), F32)


def _attn(q, k, v, k_ctx, v_ctx, sink, local):
    B, N, _ = q.shape
    C = k_ctx.shape[1]
    blk = SWA_BLOCK
    nb = N // blk
    qspec = pl.BlockSpec((1, blk, BRANCH_WIDTH), lambda b, i: (b, i, 0))
    ctx_spec = pl.BlockSpec((1, C, LANES), lambda b, i: (b, 0, 0))
    smem = pl.BlockSpec(memory_space=pltpu.SMEM)
    if local:
        prev = pl.BlockSpec((1, blk, LANES), lambda b, i: (b, jnp.maximum(i - 1, 0), 0))
        cur = pl.BlockSpec((1, blk, LANES), lambda b, i: (b, i, 0))
        nxt = pl.BlockSpec((1, blk, LANES), lambda b, i: (b, jnp.minimum(i + 1, nb - 1), 0))
        band = _band_bias(blk, C)
        in_specs = [smem, qspec, _const_spec(band.shape), prev, cur, nxt, prev, cur, nxt, ctx_spec, ctx_spec]
        args = (sink, q, band, k, k, k, v, v, v, k_ctx, v_ctx)
    else:
        in_specs = [smem, qspec, ctx_spec, ctx_spec]
        args = (sink, q, k_ctx, v_ctx)
    return pl.pallas_call(
        functools.partial(_attn_kernel, local=local),
        grid=(B, nb),
        in_specs=in_specs,
        out_specs=qspec,
        out_shape=jax.ShapeDtypeStruct((B, N, BRANCH_WIDTH), BF16),
        compiler_params=_params(("parallel", "parallel")),
        name="attn_local" if local else "attn_ctx",
    )(*args)


def _gla_kernel(kqf_ref, vf_ref, gf_ref, kqb_ref, vb_ref, gb_ref, s0_ref,
                of_ref, ob_ref, sfin_ref, st_ref):
    i = pl.program_id(1)
    T = kqf_ref.shape[1]
    n_chunk = T // GLA_CHUNK

    @pl.when(i == 0)
    def _():
        st_ref[...] = s0_ref[0]

    cum_rows = min(T, GLA_CUM_ROWS)
    ri = lax.broadcasted_iota(jnp.int32, (cum_rows, cum_rows), 0)
    ci = lax.broadcasted_iota(jnp.int32, (cum_rows, cum_rows), 1)
    same_chunk = (ri // GLA_CHUNK) == (ci // GLA_CHUNK)
    li = lax.broadcasted_iota(jnp.int32, (GLA_CHUNK, GLA_CHUNK), 0)
    mi = lax.broadcasted_iota(jnp.int32, (GLA_CHUNK, GLA_CHUNK), 1)
    first_head_lanes = lax.broadcasted_iota(jnp.int32, (GLA_CHUNK, LANES), 1) < GLA_DK

    dirs = ((kqf_ref, vf_ref, gf_ref, of_ref), (kqb_ref, vb_ref, gb_ref, ob_ref))
    ops = {}
    for d, (kq_ref, v_ref, g_ref, o_ref) in enumerate(dirs):
        backward = d == 1
        tri_full = jnp.where(same_chunk & ((ci >= ri) if backward else (ci <= ri)), 1.0, 0.0).astype(BF16)
        cum = jnp.concatenate([_split_dot_left(tri_full, g_ref[0, r0:r0 + cum_rows])
                               for r0 in range(0, T, cum_rows)], axis=0)
        ref_row = GLA_CHUNK // 2 - 1 if backward else GLA_CHUNK // 2
        tot_row = 0 if backward else GLA_CHUNK - 1
        for p in range(GLA_HEADS // 2):
            lanes = slice(p * LANES, (p + 1) * LANES)
            k_lanes = slice(GLA_HEADS * GLA_DK + p * LANES, GLA_HEADS * GLA_DK + (p + 1) * LANES)
            for c in range(n_chunk):
                rows = slice(c * GLA_CHUNK, (c + 1) * GLA_CHUNK)
                bc = cum[rows, lanes]
                q2 = kq_ref[0, rows, lanes]
                k2 = kq_ref[0, rows, k_lanes]
                b_ref = bc[ref_row:ref_row + 1]
                b_tot = bc[tot_row:tot_row + 1]
                qe = (q2 * jnp.exp(bc - b_ref)).astype(BF16)
                q_in = (q2 * jnp.exp(bc)).astype(BF16)
                ke = k2 * jnp.exp(b_ref - bc)
                k_out = k2 * jnp.exp(b_tot - bc)
                decay = jnp.exp(b_tot)
                for hh in range(2):
                    own = first_head_lanes if hh == 0 else ~first_head_lanes
                    ops[d, 2 * p + hh, c] = dict(
                        qe=qe, q_in=q_in, decay=decay,
                        ke=jnp.where(own, ke, 0.0).astype(BF16),
                        k_out=jnp.where(own, k_out, 0.0).astype(BF16))

    for (d, h, c), u in ops.items():
        keep = (mi >= li) if d == 1 else (mi <= li)
        u["pm"] = jnp.where(keep, _dot_nt(u["qe"], u["ke"]), 0.0).astype(BF16)

    for step in range(n_chunk):
        for d, (kq_ref, v_ref, g_ref, o_ref) in enumerate(dirs):
            c = n_chunk - 1 - step if d == 1 else step
            rows = slice(c * GLA_CHUNK, (c + 1) * GLA_CHUNK)
            for h in range(GLA_HEADS):
                lanes = slice(h * LANES, (h + 1) * LANES)
                u = ops[d, h, c]
                vch = v_ref[0, rows, lanes]
                st = st_ref[d, h]
                o = _dot(u["pm"], vch) + _dot_nt(u["q_in"], st.astype(BF16))
                o_ref[0, rows, lanes] = o.astype(o_ref.dtype)
                st_ref[d, h] = st * u["decay"] + _dot_tn(vch, u["k_out"])

    @pl.when(i == pl.num_programs(1) - 1)
    def _():
        sfin_ref[0] = st_ref[...]


def _split_dot_left(m, a):
    hi = a.astype(BF16)
    lo = (a - hi.astype(F32)).astype(BF16)
    return _dot(m, hi) + _dot(m, lo)


def _gla(kq, v, g2, s0, T):
    B, N, _ = kq.shape
    nt = N // T
    fwd = lambda w: pl.BlockSpec((1, T, w), lambda b, i: (b, i, 0))
    bwd = lambda w: pl.BlockSpec((1, T, w), lambda b, i: (b, nt - 1 - i, 0))
    g_fwd = pl.BlockSpec((1, T, 256), lambda b, i: (b, i, 0))
    g_bwd = pl.BlockSpec((1, T, 256), lambda b, i: (b, nt - 1 - i, 1))
    st_shape = (2, GLA_HEADS, GLA_DV, LANES)
    st_spec = pl.BlockSpec((1,) + st_shape, lambda b, i: (b, 0, 0, 0, 0))
    return pl.pallas_call(
        _gla_kernel,
        grid=(B, nt),
        in_specs=[fwd(512), fwd(512), g_fwd, bwd(512), bwd(512), g_bwd, st_spec],
        out_specs=[fwd(512), bwd(512), st_spec],
        out_shape=[jax.ShapeDtypeStruct((B, N, 512), BF16), jax.ShapeDtypeStruct((B, N, 512), BF16),
                   jax.ShapeDtypeStruct((B,) + st_shape, F32)],
        scratch_shapes=[pltpu.VMEM(st_shape, F32)],
        compiler_params=_params(("parallel", "arbitrary")),
        name="gla",
    )(kq, v, g2, kq, v, g2, s0)


def _merge_kernel(x_ref, sh_ref, sc_ref, g1_ref, a_ref, b_ref, of_ref, ob_ref, r_ref,
                  gn_ref, wg_ref, wbr_ref, wo_ref, o_ref):
    sub_tiles = range(0, x_ref.shape[1], PROJ_SUBTILE)
    mixes = []
    for r0 in sub_tiles:
        rows = slice(r0, r0 + PROJ_SUBTILE)
        hb = _modulated(x_ref[0, rows], sh_ref[0], sc_ref[0]).astype(BF16)
        o = of_ref[0, rows].astype(F32) + ob_ref[0, rows].astype(F32)
        cs = []
        for h in range(GLA_HEADS):
            oh = o[:, h * LANES:(h + 1) * LANES]
            cs.append(oh * lax.rsqrt(jnp.mean(oh * oh, axis=-1, keepdims=True) + EPS) * gn_ref[...])
        r = r_ref[0, rows].astype(F32)
        c_out = (jnp.concatenate(cs, axis=1) * (r * jax.nn.sigmoid(r))).astype(BF16)
        branches = (a_ref[0, rows], b_ref[0, rows], c_out)
        mixed = None
        for k in range(N_BRANCH):
            gate = jax.nn.sigmoid(_dot(hb, wg_ref[:, k * D_MODEL:(k + 1) * D_MODEL])).astype(BF16)
            term = gate.astype(F32) * _dot(branches[k], wbr_ref[k])
            mixed = term if mixed is None else mixed + term
        mixes.append(mixed.astype(BF16))
    for r0, mixed in zip(sub_tiles, mixes):
        rows = slice(r0, r0 + PROJ_SUBTILE)
        o_ref[0, rows] = x_ref[0, rows] + g1_ref[0] * _dot(mixed, wo_ref[...])


def _merge(x, shift, scale, g1, a, b, o_f, o_b, r, lw, tm):
    B, N, _ = x.shape
    tok = lambda w: pl.BlockSpec((1, tm, w), lambda b_, i: (b_, i, 0))
    mod = pl.BlockSpec((1, 1, D_MODEL), lambda b_, i: (b_, 0, 0))
    consts = [lw["gla_norm"], lw["w_g"], lw["w_br"], lw["w_o"]]
    return pl.pallas_call(
        _merge_kernel,
        grid=(B, N // tm),
        in_specs=[tok(D_MODEL), mod, mod, mod, tok(512), tok(512), tok(512), tok(512), tok(512)]
                 + [_const_spec(c.shape) for c in consts],
        out_specs=tok(D_MODEL),
        out_shape=jax.ShapeDtypeStruct((B, N, D_MODEL), F32),
        compiler_params=_params(("parallel", "parallel")),
        name="merge",
    )(x, shift, scale, g1, a, b, o_f, o_b, r, *consts)


def _ffn_kernel(x_ref, sh_ref, sc_ref, g2_ref, w1_ref, w2_ref, o_ref):
    x = x_ref[0]
    hb = _modulated(x, sh_ref[0], sc_ref[0]).astype(BF16)
    t = jnp.maximum(_dot(hb, w1_ref[...]), 0.0)
    f = _dot((t * t).astype(BF16), w2_ref[...])
    o_ref[0] = x + g2_ref[0] * f


def _ffn(x, shift, scale, gate, lw, tm):
    B, N, _ = x.shape
    tok = pl.BlockSpec((1, tm, D_MODEL), lambda b, i: (b, i, 0))
    mod = pl.BlockSpec((1, 1, D_MODEL), lambda b, i: (b, 0, 0))
    return pl.pallas_call(
        _ffn_kernel,
        grid=(B, N // tm),
        in_specs=[tok, mod, mod, mod, _const_spec(lw["w_ff1"].shape), _const_spec(lw["w_ff2"].shape)],
        out_specs=tok,
        out_shape=jax.ShapeDtypeStruct((B, N, D_MODEL), F32),
        compiler_params=_params(("parallel", "parallel")),
        name="ffn",
    )(x, shift, scale, gate, lw["w_ff1"], lw["w_ff2"])


def _rope_tables(n_tokens):
    lane = np.arange(LANES)
    d = lane % HEAD_DIM
    use_col = (d // (HEAD_DIM // 2)) == 1
    second = ((d % (HEAD_DIM // 2)) // (HEAD_DIM // 4)) == 1
    f = d % (HEAD_DIM // 4)
    n_freq = HEAD_DIM // 4
    freqs = ROPE_BASE ** (-jnp.arange(n_freq, dtype=F32) / n_freq)
    t = jnp.arange(n_tokens)
    row = (t // GRID_W).astype(F32)
    col = (t % GRID_W).astype(F32)
    ang_row = row[:, None] * freqs[None, :]
    ang_col = col[:, None] * freqs[None, :]
    ang = jnp.where(jnp.asarray(use_col)[None, :], ang_col[:, f], ang_row[:, f])
    cos, sin = jnp.cos(ang), jnp.sin(ang)
    second = jnp.asarray(second)[None, :]
    return cos, jnp.where(second, sin, 0.0), jnp.where(second, 0.0, -sin)


def _identity_tables(n_tokens):
    return (jnp.ones((n_tokens, LANES), F32), jnp.zeros((n_tokens, LANES), F32),
            jnp.zeros((n_tokens, LANES), F32))


def _layer_weights(l, w_in, q_norm, k_norm, sgu_norm, w_sgu, b_sgu, w_gate_f, b_gate_f,
                   w_gate_b, b_gate_b, gla_norm, w_br, w_o, w_ff1, w_ff2):
    w = w_in[l]
    col = lambda i: w[:, _OFFS[i]:_OFFS[i + 1]]
    wq = col(2).reshape(D_MODEL, SWA_KV_HEADS, SWA_GROUP, HEAD_DIM).transpose(0, 2, 1, 3).reshape(D_MODEL, 512)
    w_z = jnp.zeros((D_MODEL, LANES), F32).at[:, :2 * GLA_RANK].set(w[:, _OFFS[9]:_OFFS[11]])
    n_dk = GLA_HEADS * GLA_DK
    w2 = jnp.zeros((LANES, 2 * n_dk), F32)
    w2 = w2.at[:GLA_RANK, :n_dk].set(w_gate_f[l]).at[GLA_RANK:2 * GLA_RANK, n_dk:].set(w_gate_b[l])
    b2 = jnp.concatenate([b_gate_f[l], b_gate_b[l]])[None, :]
    wbr1 = w_br[l, 1].reshape(SWA_KV_HEADS, SWA_GROUP, HEAD_DIM, D_MODEL).transpose(1, 0, 2, 3).reshape(512, D_MODEL)
    kq_scale = jnp.concatenate([jnp.full((n_dk,), GLA_DK ** -0.5, F32), jnp.ones((n_dk,), F32)])[None, :]
    return {
        "w_uv": w[:, _OFFS[0]:_OFFS[2]].astype(BF16),
        "w_b": jnp.concatenate([wq, col(3), col(4)], axis=1).astype(BF16),
        "w_c": w[:, _OFFS[5]:_OFFS[9]].astype(BF16),
        "w_z": w_z.astype(BF16),
        "w_g": col(11).astype(BF16),
        "w2": w2.astype(BF16),
        "b2": b2,
        "sgu_norm": sgu_norm[l][None, :],
        "w_sgu": w_sgu[l].astype(BF16),
        "b_sgu": jnp.broadcast_to(b_sgu[l][:, :, None], (SGU_GROUPS, SGU_CHUNK, LANES)),
        "q_norm": jnp.tile(q_norm[l], 2)[None, :],
        "k_norm": jnp.tile(k_norm[l], 2)[None, :],
        "kq_scale": kq_scale,
        "gla_norm": gla_norm[l][None, :],
        "w_br": jnp.stack([w_br[l, 0], wbr1, w_br[l, 2]]).astype(BF16),
        "w_o": w_o[l].astype(BF16),
        "w_ff1": w_ff1[l].astype(BF16),
        "w_ff2": w_ff2[l].astype(BF16),
    }


def kernel(x, c, ctx, c_ctx, w_ada, b_ada, w_in, q_norm, k_norm, sink, sgu_norm, w_sgu, b_sgu,
           w_gate_f, b_gate_f, w_gate_b, b_gate_b, gla_norm, w_br, w_o, w_ff1, w_ff2):
    B, N, _ = x.shape
    C = ctx.shape[1]
    depth = w_ada.shape[0]
    tm_lat, tm_ctx = 512, C
    gla_tile = 512

    cond = jnp.zeros((16, D_MODEL), F32).at[:B].set(c).at[B].set(c_ctx)
    mods = _adaln(cond, w_ada, b_ada)
    rope_lat = _rope_tables(N)
    rope_ctx = _identity_tables(C)
    zero_state = jnp.zeros((B, 2, GLA_HEADS, GLA_DV, LANES), F32)

    for l in range(depth):
        lw = _layer_weights(l, w_in, q_norm, k_norm, sgu_norm, w_sgu, b_sgu, w_gate_f, b_gate_f,
                            w_gate_b, b_gate_b, gla_norm, w_br, w_o, w_ff1, w_ff2)
        m = mods[l].reshape(16, 6, D_MODEL)
        lat = [m[:B, j][:, None, :] for j in range(6)]
        cm = [jnp.broadcast_to(m[B, j][None, None, :], (B, 1, D_MODEL)) for j in range(6)]

        (ca, cq, ck, cv, ckq, cvc, cr, cg2) = _proj(ctx, cm[0], cm[1], rope_ctx, lw, tm_ctx)
        oc_f, oc_b, s_ctx = _gla(ckq, cvc, cg2, zero_state, min(gla_tile, C))

        (a, q, k, v, kq, vc, r, g2) = _proj(x, lat[0], lat[1], rope_lat, lw, tm_lat)
        b_out = _attn(q, k, v, ck, cv, sink[l], local=True)
        o_f, o_b, _ = _gla(kq, vc, g2, s_ctx, gla_tile)
        x_new = _merge(x, lat[0], lat[1], lat[2], a, b_out, o_f, o_b, r, lw, tm_lat)
        x_new = _ffn(x_new, lat[3], lat[4], lat[5], lw, tm_lat)

        if l < depth - 1:
            cb_out = _attn(cq, None, None, ck, cv, sink[l], local=False)
            ctx_new = _merge(ctx, cm[0], cm[1], cm[2], ca, cb_out, oc_f, oc_b, cr, lw, tm_ctx)
            ctx = _ffn(ctx_new, cm[3], cm[4], cm[5], lw, tm_ctx)
        x = x_new

    return x
```

```python
import functools

import jax
import jax.numpy as jnp
import numpy as np
from jax import lax
from jax.experimental import pallas as pl
from jax.experimental.pallas import tpu as pltpu

D_MODEL = 1024
GRID_W = 64
HEAD_DIM = 64
BRANCH_WIDTH = 512
N_BRANCH = 3
SGU_GROUPS = 4
SGU_CHUNK = 128
SWA_Q_HEADS = 8
SWA_KV_HEADS = 2
SWA_GROUP = 4
SWA_WINDOW = 128
SWA_BLOCK = 128
ROPE_BASE = 10000.0
GLA_HEADS = 4
GLA_DK = 64
GLA_DV = 128
GLA_RANK = 16
GLA_NORMALIZER = 16.0
GLA_CHUNK = 64
FFN_DIM = 4 * D_MODEL
EPS = 1e-6

LANES = 128
VMEM_LIMIT = 56 * 1024 * 1024
NEG_BIG = -1e30
LOG2E = 1.4426950408889634
PROJ_SUBTILE = 256
GLA_CUM_ROWS = 256
ATTN_Q_BLOCKS = 8
GLA_GROUP_STEPS = 1
GLA_STATE_SHAPE = (2, GLA_HEADS // 2, 2 * GLA_DV, LANES)
BF16 = jnp.bfloat16
F32 = jnp.float32

_SPLITS = (512, 512, 512, 128, 128, 256, 256, 512, 512, 16, 16, 3072)
_OFFS = np.concatenate([[0], np.cumsum(_SPLITS)]).tolist()


def _dot(a, b):
    return jnp.dot(a, b, preferred_element_type=F32)


def _dot_nt(a, b):
    return lax.dot_general(a, b, (((1,), (1,)), ((), ())), preferred_element_type=F32)


def _dot_tn(a, b):
    return lax.dot_general(a, b, (((0,), (0,)), ((), ())), preferred_element_type=F32)


def _const_spec(shape):
    nd = len(shape)
    return pl.BlockSpec(shape, lambda *_: (0,) * nd, pipeline_mode=pl.Buffered(1))


def _params(sem):
    return pltpu.CompilerParams(dimension_semantics=sem, vmem_limit_bytes=VMEM_LIMIT)


def _adaln_kernel(cond_ref, w_ref, b_ref, o_ref):
    cond = cond_ref[...]
    s = (cond * jax.nn.sigmoid(cond)).astype(BF16)
    o_ref[0] = _dot(s, w_ref[0].astype(BF16)) + b_ref[0]


def _adaln(cond, w_ada, b_ada):
    depth = w_ada.shape[0]
    rows = cond.shape[0]
    tn = 1536
    return pl.pallas_call(
        _adaln_kernel,
        grid=(depth, 6 * D_MODEL // tn),
        in_specs=[
            pl.BlockSpec((rows, D_MODEL), lambda l, j: (0, 0)),
            pl.BlockSpec((1, D_MODEL, tn), lambda l, j: (l, 0, j)),
            pl.BlockSpec((1, 1, tn), lambda l, j: (l, 0, j)),
        ],
        out_specs=pl.BlockSpec((1, rows, tn), lambda l, j: (l, 0, j)),
        out_shape=jax.ShapeDtypeStruct((depth, rows, 6 * D_MODEL), F32),
        compiler_params=_params(("arbitrary", "arbitrary")),
        name="adaln",
    )(cond, w_ada, b_ada.reshape(depth, 1, 6 * D_MODEL))


def _modulated(x, shift, scale):
    ms = jnp.mean(x * x, axis=-1, keepdims=True)
    return x * lax.rsqrt(ms + EPS) * (1.0 + scale) + shift


def _proj_kernel(x_ref, sh_ref, sc_ref, cos_ref, sa_ref, sb_ref,
                 wuv_ref, wb_ref, wc_ref, wz_ref, w2_ref, b2_ref,
                 sgun_ref, ws_ref, bs_ref, qn_ref, kn_ref,
                 a_ref, q_ref, k_ref, v_ref, kq_ref, vc_ref, r_ref, g2_ref):
    ri = lax.broadcasted_iota(jnp.int32, (2 * LANES, 2 * LANES), 0) // HEAD_DIM
    ci = lax.broadcasted_iota(jnp.int32, (2 * LANES, 2 * LANES), 1) // HEAD_DIM
    group_ones = jnp.where(ri == ci, 1.0, 0.0).astype(BF16)

    tiles = [slice(r0, r0 + PROJ_SUBTILE) for r0 in range(0, x_ref.shape[1], PROJ_SUBTILE)]

    hbs = {rows.start: _modulated(x_ref[0, rows], sh_ref[0], sc_ref[0]).astype(BF16) for rows in tiles}

    def big(rows):
        hb = hbs[rows.start]
        return dict(uv=_dot(hb, wuv_ref[...]), bq=_dot(hb, wb_ref[...]),
                    z=_dot(hb, wz_ref[...]).astype(BF16), cp=_dot(hb, wc_ref[...]))

    def prepare(rows, t):
        cp, bq, uv = t["cp"], t["bq"], t["uv"]
        n_q = GLA_HEADS * GLA_DK
        kq_ref[0, rows, :n_q] = cp[:, :n_q] * GLA_DK ** -0.5
        kq_ref[0, rows, n_q:] = cp[:, n_q:512]
        vc_ref[0, rows] = cp[:, 512:1024].astype(vc_ref.dtype)
        r_ref[0, rows] = cp[:, 1024:1536].astype(r_ref.dtype)
        v_ref[0, rows] = bq[:, 640:768].astype(v_ref.dtype)
        t["sq"] = [(xs * xs).astype(BF16) for xs in (bq[:, :256], bq[:, 256:512], bq[:, 512:768])]
        t["u"] = jax.nn.gelu(uv[:, :BRANCH_WIDTH])
        va = jax.nn.gelu(uv[:, BRANCH_WIDTH:])
        t["vg"] = []
        for g in range(SGU_GROUPS):
            lanes = slice(g * LANES, (g + 1) * LANES)
            vg = va[:, lanes]
            vg = vg * lax.rsqrt(jnp.mean(vg * vg, axis=-1, keepdims=True) + EPS) * sgun_ref[:, lanes]
            t["vg"].append(vg.astype(BF16))

    def small(rows, t):
        bq = t["bq"]
        cos, sa, sb = cos_ref[rows], sa_ref[rows], sb_ref[rows]

        def rope(y):
            return y * cos + pltpu.roll(y, 16, 1) * sa + pltpu.roll(y, LANES - 16, 1) * sb

        logit = _dot(t["z"], w2_ref[...]) + b2_ref[...]
        log_sig = jnp.minimum(logit, 0.0) - jnp.log(1.0 + jnp.exp(-jnp.abs(logit)))
        g2_ref[0, rows] = log_sig * (1.0 / GLA_NORMALIZER)
        inv = [lax.rsqrt(_dot(sq, group_ones) * (1.0 / HEAD_DIM) + EPS) for sq in t["sq"]]
        for s in range(BRANCH_WIDTH // LANES):
            lanes = slice(s * LANES, (s + 1) * LANES)
            qn = bq[:, lanes] * inv[s // 2][:, (s % 2) * LANES:(s % 2 + 1) * LANES] * qn_ref[...]
            q_ref[0, rows, lanes] = (rope(qn) * (HEAD_DIM ** -0.5 * LOG2E)).astype(q_ref.dtype)
        k_ref[0, rows] = rope(bq[:, 512:640] * inv[2][:, :LANES] * kn_ref[...]).astype(k_ref.dtype)
        n_chunk = PROJ_SUBTILE // SGU_CHUNK
        for g in range(SGU_GROUPS):
            lanes = slice(g * LANES, (g + 1) * LANES)
            vg = jnp.concatenate([t["vg"][g][c * SGU_CHUNK:(c + 1) * SGU_CHUNK] for c in range(n_chunk)], axis=1)
            s_all = _dot(ws_ref[g], vg)
            for c in range(n_chunk):
                cr = slice(c * SGU_CHUNK, (c + 1) * SGU_CHUNK)
                out_rows = slice(rows.start + c * SGU_CHUNK, rows.start + (c + 1) * SGU_CHUNK)
                s = s_all[:, c * LANES:(c + 1) * LANES] + bs_ref[g]
                a_ref[0, out_rows, lanes] = (t["u"][cr, lanes] * s).astype(a_ref.dtype)

    prev = None
    for rows in tiles:
        cur = big(rows)
        if prev is not None:
            small(*prev)
        prepare(rows, cur)
        prev = (rows, cur)
    small(*prev)


def _proj(x, shift, scale, rope, lw, tm):
    B, N, _ = x.shape
    cos, sa, sb = rope
    tok = lambda w: pl.BlockSpec((1, tm, w), lambda b, i: (b, i, 0))
    mod = pl.BlockSpec((1, 1, D_MODEL), lambda b, i: (b, 0, 0))
    tab = pl.BlockSpec((tm, LANES), lambda b, i: (i, 0))
    consts = [lw["w_uv"], lw["w_b"], lw["w_c"], lw["w_z"], lw["w2"], lw["b2"],
              lw["sgu_norm"], lw["w_sgu"], lw["b_sgu"], lw["q_norm"], lw["k_norm"]]
    out_w = [(512, BF16), (512, BF16), (128, BF16), (128, BF16), (512, F32), (512, BF16),
             (512, BF16), (512, F32)]
    return pl.pallas_call(
        _proj_kernel,
        grid=(B, N // tm),
        in_specs=[tok(D_MODEL), mod, mod, tab, tab, tab] + [_const_spec(c.shape) for c in consts],
        out_specs=[tok(w) for w, _ in out_w],
        out_shape=[jax.ShapeDtypeStruct((B, N, w), dt) for w, dt in out_w],
        compiler_params=_params(("parallel", "parallel")),
        name="proj",
    )(x, shift, scale, cos, sa, sb, *consts)


def _attn_kernel(sink_ref, q_ref, *refs, local):
    blk = SWA_BLOCK
    n_q = q_ref.shape[1] // blk
    n_slab = BRANCH_WIDTH // LANES
    if local:
        band_ref, kp_ref, kc_ref, kn_ref, vp_ref, vc_ref, vn_ref, kx_ref, vx_ref, o_ref = refs
        k_loc = [kp_ref[0]] + [kc_ref[0, j * blk:(j + 1) * blk] for j in range(n_q)] + [kn_ref[0]]
        v_loc = [vp_ref[0]] + [vc_ref[0, j * blk:(j + 1) * blk] for j in range(n_q)] + [vn_ref[0]]
        keys = [jnp.concatenate(k_loc[j:j + 3] + [kx_ref[0]], axis=0) for j in range(n_q)]
        vals = [jnp.concatenate(v_loc[j:j + 3] + [vx_ref[0]], axis=0) for j in range(n_q)]
        i = pl.program_id(1)
        no_prev = jnp.where(i == 0, NEG_BIG, 0.0)
        no_next = jnp.where(i == pl.num_programs(1) - 1, NEG_BIG, 0.0)
        bias_prev = [band_ref[:, :blk] + no_prev if j == 0 else band_ref[:, :blk] for j in range(n_q)]
        bias_next = [band_ref[:, blk:] + no_next if j == n_q - 1 else band_ref[:, blk:] for j in range(n_q)]
    else:
        kx_ref, vx_ref, o_ref = refs
        keys, vals = [kx_ref[0]] * n_q, [vx_ref[0]] * n_q
    head0_lanes = lax.broadcasted_iota(jnp.int32, (blk, LANES), 1) < HEAD_DIM
    val_lane = lax.broadcasted_iota(jnp.int32, vals[0].shape, 1)

    scores = {}

    def score_block(j):
        rows = slice(j * blk, (j + 1) * blk)
        for h in range(SWA_KV_HEADS):
            own = head0_lanes if h == 0 else ~head0_lanes
            qh = [jnp.where(own, q_ref[0, rows, s * LANES:(s + 1) * LANES], 0.0).astype(BF16)
                  for s in range(n_slab)]
            scores[j, h] = _dot_nt(jnp.concatenate(qh, axis=0), keys[j])

    score_block(0)
    for j in range(n_q):
        if j + 1 < n_q:
            score_block(j + 1)
        outs = {}
        for h in range(SWA_KV_HEADS):
            ps, sink_terms = [], []
            for s in range(n_slab):
                sc = scores[j, h][s * blk:(s + 1) * blk]
                if local:
                    sc = jnp.concatenate([sc[:, :blk] + bias_prev[j], sc[:, blk:2 * blk],
                                          sc[:, 2 * blk:3 * blk] + bias_next[j], sc[:, 3 * blk:]], axis=1)
                sink = sink_ref[h * SWA_GROUP + s] * LOG2E
                m = jnp.maximum(jnp.max(sc, axis=-1, keepdims=True), sink)
                ps.append(jnp.exp2(sc - m).astype(BF16))
                sink_terms.append(jnp.exp2(sink - m))
            own = (val_lane < HEAD_DIM) if h == 0 else (val_lane >= HEAD_DIM)
            pv = _dot(jnp.concatenate(ps, axis=0), jnp.where(own, vals[j], jnp.ones_like(vals[j])))
            for s in range(n_slab):
                pv_s = pv[s * blk:(s + 1) * blk]
                denom = pltpu.roll(pv_s, HEAD_DIM, 1) + sink_terms[s]
                outs[s, h] = pv_s / denom
        for s in range(n_slab):
            o_ref[0, j * blk:(j + 1) * blk, s * LANES:(s + 1) * LANES] = jnp.where(
                head0_lanes, outs[s, 0], outs[s, 1]).astype(o_ref.dtype)


def _band_bias(blk):
    qi = np.arange(blk)[:, None]
    kj = np.arange(blk)[None, :]
    prev_ok = np.abs(kj - blk - qi) <= SWA_WINDOW
    next_ok = np.abs(kj + blk - qi) <= SWA_WINDOW
    return jnp.asarray(np.where(np.concatenate([prev_ok, next_ok], axis=1), 0.0, NEG_BIG), F32)


def _attn(q, k, v, k_ctx, v_ctx, sink, local):
    B, N, _ = q.shape
    C = k_ctx.shape[1]
    blk = SWA_BLOCK
    nb = N // blk
    n_q = min(ATTN_Q_BLOCKS, nb)
    steps = nb // n_q
    qspec = pl.BlockSpec((1, n_q * blk, BRANCH_WIDTH), lambda b, i: (b, i, 0))
    ctx_spec = pl.BlockSpec((1, C, LANES), lambda b, i: (b, 0, 0))
    smem = pl.BlockSpec(memory_space=pltpu.SMEM)
    if local:
        prev = pl.BlockSpec((1, blk, LANES), lambda b, i: (b, jnp.maximum(n_q * i - 1, 0), 0))
        cur = pl.BlockSpec((1, n_q * blk, LANES), lambda b, i: (b, i, 0))
        nxt = pl.BlockSpec((1, blk, LANES), lambda b, i: (b, jnp.minimum(n_q * (i + 1), nb - 1), 0))
        band = _band_bias(blk)
        in_specs = [smem, qspec, _const_spec(band.shape), prev, cur, nxt, prev, cur, nxt, ctx_spec, ctx_spec]
        args = (sink, q, band, k, k, k, v, v, v, k_ctx, v_ctx)
    else:
        in_specs = [smem, qspec, ctx_spec, ctx_spec]
        args = (sink, q, k_ctx, v_ctx)
    return pl.pallas_call(
        functools.partial(_attn_kernel, local=local),
        grid=(B, steps),
        in_specs=in_specs,
        out_specs=qspec,
        out_shape=jax.ShapeDtypeStruct((B, N, BRANCH_WIDTH), BF16),
        compiler_params=_params(("parallel", "parallel")),
        name="attn_local" if local else "attn_ctx",
    )(*args)


def _gla_kernel(kqf_ref, vf_ref, gf_ref, kqb_ref, vb_ref, gb_ref, s0_ref,
                of_ref, ob_ref, sfin_ref, st_ref):
    i = pl.program_id(1)
    T = kqf_ref.shape[1]
    n_chunk = T // GLA_CHUNK

    @pl.when(i == 0)
    def _():
        st_ref[...] = s0_ref[0]

    cum_rows = min(T, GLA_CUM_ROWS)
    ri = lax.broadcasted_iota(jnp.int32, (cum_rows, cum_rows), 0)
    ci = lax.broadcasted_iota(jnp.int32, (cum_rows, cum_rows), 1)
    same_chunk = (ri // GLA_CHUNK) == (ci // GLA_CHUNK)
    li = lax.broadcasted_iota(jnp.int32, (GLA_CHUNK, 2 * GLA_CHUNK), 0)
    mi = lax.broadcasted_iota(jnp.int32, (GLA_CHUNK, 2 * GLA_CHUNK), 1) % GLA_CHUNK
    first_head_lanes = lax.broadcasted_iota(jnp.int32, (GLA_CHUNK, LANES), 1) < GLA_DK
    own_block = ((lax.broadcasted_iota(jnp.int32, (2 * GLA_DV, LANES), 0) < GLA_DV)
                 == (lax.broadcasted_iota(jnp.int32, (2 * GLA_DV, LANES), 1) < GLA_DK))

    dirs = ((kqf_ref, vf_ref, gf_ref, of_ref), (kqb_ref, vb_ref, gb_ref, ob_ref))
    cums = []
    for d, (kq_ref, v_ref, g_ref, o_ref) in enumerate(dirs):
        tri_full = jnp.where(same_chunk & ((ci >= ri) if d == 1 else (ci <= ri)), 1.0, 0.0).astype(BF16)
        cums.append(jnp.concatenate([_split_dot_left(tri_full, g_ref[0, r0:r0 + cum_rows])
                                     for r0 in range(0, T, cum_rows)], axis=0))

    def chunk_of(d, step):
        return n_chunk - 1 - step if d == 1 else step

    def operands(d, p, c, anchor):
        kq_ref = dirs[d][0]
        lanes = slice(p * LANES, (p + 1) * LANES)
        k_lanes = slice(GLA_HEADS * GLA_DK + p * LANES, GLA_HEADS * GLA_DK + (p + 1) * LANES)
        rows = slice(c * GLA_CHUNK, (c + 1) * GLA_CHUNK)
        ref_row = GLA_CHUNK // 2 - 1 if d == 1 else GLA_CHUNK // 2
        tot_row = 0 if d == 1 else GLA_CHUNK - 1
        bc = cums[d][rows, lanes]
        if anchor is not None:
            bc = bc + anchor
        q2 = kq_ref[0, rows, lanes]
        k2 = kq_ref[0, rows, k_lanes]
        b_ref = bc[ref_row:ref_row + 1]
        b_tot = bc[tot_row:tot_row + 1]
        ke = k2 * jnp.exp(b_ref - bc)
        return dict(
            qe=(q2 * jnp.exp(bc - b_ref)).astype(BF16),
            q_in=(q2 * jnp.exp(bc)).astype(BF16),
            k_out=(k2 * jnp.exp(b_tot - bc)).astype(BF16),
            decay=jnp.exp(b_tot),
            ke=jnp.concatenate([jnp.where(first_head_lanes, ke, 0.0),
                                jnp.where(first_head_lanes, 0.0, ke)], axis=0).astype(BF16))

    ops = {}
    anchors = {}

    def score_group(gi, steps):
        anchor = anchors.get(gi - 2)
        for step in steps:
            for d in range(2):
                for p in range(GLA_HEADS // 2):
                    u = operands(d, p, chunk_of(d, step), anchor)
                    keep = (mi >= li) if d == 1 else (mi <= li)
                    scores = _dot_nt(u["qe"], u["ke"])
                    u["pm"] = jnp.where(keep, scores, 0.0).astype(BF16)
                    ops[d, p, step] = u
        anchors[gi] = jnp.minimum(jnp.abs(scores[:1]), 0.0)

    zeros_v = jnp.zeros((GLA_CHUNK, GLA_DV), BF16)

    def scan_group(steps):
        for step in steps:
            for d, (kq_ref, v_ref, g_ref, o_ref) in enumerate(dirs):
                c = chunk_of(d, step)
                rows = slice(c * GLA_CHUNK, (c + 1) * GLA_CHUNK)
                for p in range(GLA_HEADS // 2):
                    lanes = slice(p * 2 * GLA_DV, (p + 1) * 2 * GLA_DV)
                    u = ops.pop((d, p, step))
                    v2 = v_ref[0, rows, lanes]
                    v_diag = jnp.concatenate(
                        [jnp.concatenate([v2[:, :GLA_DV], zeros_v], axis=1),
                         jnp.concatenate([zeros_v, v2[:, GLA_DV:]], axis=1)], axis=0)
                    st = st_ref[d, p]
                    o = _dot(u["pm"], v_diag) + _dot_nt(u["q_in"], st.astype(BF16))
                    o_ref[0, rows, lanes] = o.astype(o_ref.dtype)
                    st_ref[d, p] = st * u["decay"] + jnp.where(own_block, _dot_tn(v2, u["k_out"]), 0.0)

    groups = [range(s, min(s + GLA_GROUP_STEPS, n_chunk)) for s in range(0, n_chunk, GLA_GROUP_STEPS)]
    score_group(0, groups[0])
    for gi, steps in enumerate(groups):
        if gi + 1 < len(groups):
            score_group(gi + 1, groups[gi + 1])
        scan_group(steps)

    @pl.when(i == pl.num_programs(1) - 1)
    def _():
        sfin_ref[0] = st_ref[...]


def _split_dot_left(m, a):
    hi = a.astype(BF16)
    lo = (a - hi.astype(F32)).astype(BF16)
    return _dot(m, hi) + _dot(m, lo)


def _gla(kq, v, g2, s0, T):
    B, N, _ = kq.shape
    nt = N // T
    fwd = lambda w: pl.BlockSpec((1, T, w), lambda b, i: (b, i, 0))
    bwd = lambda w: pl.BlockSpec((1, T, w), lambda b, i: (b, nt - 1 - i, 0))
    g_fwd = pl.BlockSpec((1, T, 256), lambda b, i: (b, i, 0))
    g_bwd = pl.BlockSpec((1, T, 256), lambda b, i: (b, nt - 1 - i, 1))
    st_shape = GLA_STATE_SHAPE
    st_spec = pl.BlockSpec((1,) + st_shape, lambda b, i: (b, 0, 0, 0, 0))
    return pl.pallas_call(
        _gla_kernel,
        grid=(B, nt),
        in_specs=[fwd(512), fwd(512), g_fwd, bwd(512), bwd(512), g_bwd, st_spec],
        out_specs=[fwd(512), bwd(512), st_spec],
        out_shape=[jax.ShapeDtypeStruct((B, N, 512), BF16), jax.ShapeDtypeStruct((B, N, 512), BF16),
                   jax.ShapeDtypeStruct((B,) + st_shape, F32)],
        scratch_shapes=[pltpu.VMEM(st_shape, F32)],
        compiler_params=_params(("parallel", "arbitrary")),
        name="gla",
    )(kq, v, g2, kq, v, g2, s0)


def _merge_kernel(x_ref, sh_ref, sc_ref, g1_ref, a_ref, b_ref, of_ref, ob_ref, r_ref,
                  gn_ref, wg_ref, wbr_ref, wo_ref, o_ref):
    def gated_mix(rows):
        hb = _modulated(x_ref[0, rows], sh_ref[0], sc_ref[0]).astype(BF16)
        o = of_ref[0, rows].astype(F32) + ob_ref[0, rows].astype(F32)
        cs = []
        for h in range(GLA_HEADS):
            oh = o[:, h * LANES:(h + 1) * LANES]
            cs.append(oh * lax.rsqrt(jnp.mean(oh * oh, axis=-1, keepdims=True) + EPS) * gn_ref[...])
        r = r_ref[0, rows].astype(F32)
        c_out = (jnp.concatenate(cs, axis=1) * (r * jax.nn.sigmoid(r))).astype(BF16)
        branches = (a_ref[0, rows], b_ref[0, rows], c_out)
        mixed = None
        for k in range(N_BRANCH):
            gate = jax.nn.sigmoid(_dot(hb, wg_ref[:, k * D_MODEL:(k + 1) * D_MODEL])).astype(BF16)
            term = gate.astype(F32) * _dot(branches[k], wbr_ref[k])
            mixed = term if mixed is None else mixed + term
        return mixed.astype(BF16)

    def project_out(rows, mixed):
        o_ref[0, rows] = x_ref[0, rows] + g1_ref[0] * _dot(mixed, wo_ref[...])

    prev = None
    for r0 in range(0, x_ref.shape[1], PROJ_SUBTILE):
        rows = slice(r0, r0 + PROJ_SUBTILE)
        mixed = gated_mix(rows)
        if prev is not None:
            project_out(*prev)
        prev = (rows, mixed)
    project_out(*prev)


def _merge(x, shift, scale, g1, a, b, o_f, o_b, r, lw, tm):
    B, N, _ = x.shape
    tok = lambda w: pl.BlockSpec((1, tm, w), lambda b_, i: (b_, i, 0))
    mod = pl.BlockSpec((1, 1, D_MODEL), lambda b_, i: (b_, 0, 0))
    consts = [lw["gla_norm"], lw["w_g"], lw["w_br"], lw["w_o"]]
    return pl.pallas_call(
        _merge_kernel,
        grid=(B, N // tm),
        in_specs=[tok(D_MODEL), mod, mod, mod, tok(512), tok(512), tok(512), tok(512), tok(512)]
                 + [_const_spec(c.shape) for c in consts],
        out_specs=tok(D_MODEL),
        out_shape=jax.ShapeDtypeStruct((B, N, D_MODEL), F32),
        compiler_params=_params(("parallel", "parallel")),
        name="merge",
    )(x, shift, scale, g1, a, b, o_f, o_b, r, *consts)


def _ffn_kernel(x_ref, sh_ref, sc_ref, g2_ref, w1_ref, w2_ref, o_ref):
    x = x_ref[0]
    hb = _modulated(x, sh_ref[0], sc_ref[0]).astype(BF16)
    t = jnp.maximum(_dot(hb, w1_ref[...]), 0.0)
    f = _dot((t * t).astype(BF16), w2_ref[...])
    o_ref[0] = x + g2_ref[0] * f


def _ffn(x, shift, scale, gate, lw, tm):
    B, N, _ = x.shape
    tok = pl.BlockSpec((1, tm, D_MODEL), lambda b, i: (b, i, 0))
    mod = pl.BlockSpec((1, 1, D_MODEL), lambda b, i: (b, 0, 0))
    return pl.pallas_call(
        _ffn_kernel,
        grid=(B, N // tm),
        in_specs=[tok, mod, mod, mod, _const_spec(lw["w_ff1"].shape), _const_spec(lw["w_ff2"].shape)],
        out_specs=tok,
        out_shape=jax.ShapeDtypeStruct((B, N, D_MODEL), F32),
        compiler_params=_params(("parallel", "parallel")),
        name="ffn",
    )(x, shift, scale, gate, lw["w_ff1"], lw["w_ff2"])


def _rope_tables(n_tokens):
    lane = np.arange(LANES)
    d = lane % HEAD_DIM
    use_col = (d // (HEAD_DIM // 2)) == 1
    second = ((d % (HEAD_DIM // 2)) // (HEAD_DIM // 4)) == 1
    f = d % (HEAD_DIM // 4)
    n_freq = HEAD_DIM // 4
    freqs = ROPE_BASE ** (-jnp.arange(n_freq, dtype=F32) / n_freq)
    t = jnp.arange(n_tokens)
    row = (t // GRID_W).astype(F32)
    col = (t % GRID_W).astype(F32)
    ang_row = row[:, None] * freqs[None, :]
    ang_col = col[:, None] * freqs[None, :]
    ang = jnp.where(jnp.asarray(use_col)[None, :], ang_col[:, f], ang_row[:, f])
    cos, sin = jnp.cos(ang), jnp.sin(ang)
    second = jnp.asarray(second)[None, :]
    return cos, jnp.where(second, sin, 0.0), jnp.where(second, 0.0, -sin)


def _identity_tables(n_tokens):
    return (jnp.ones((n_tokens, LANES), F32), jnp.zeros((n_tokens, LANES), F32),
            jnp.zeros((n_tokens, LANES), F32))


def _layer_weights(l, w_in, q_norm, k_norm, sgu_norm, w_sgu, b_sgu, w_gate_f, b_gate_f,
                   w_gate_b, b_gate_b, gla_norm, w_br, w_o, w_ff1, w_ff2):
    w = w_in[l]
    col = lambda i: w[:, _OFFS[i]:_OFFS[i + 1]]
    wq = col(2).reshape(D_MODEL, SWA_KV_HEADS, SWA_GROUP, HEAD_DIM).transpose(0, 2, 1, 3).reshape(D_MODEL, 512)
    w_z = jnp.zeros((D_MODEL, LANES), F32).at[:, :2 * GLA_RANK].set(w[:, _OFFS[9]:_OFFS[11]])
    n_dk = GLA_HEADS * GLA_DK
    w2 = jnp.zeros((LANES, 2 * n_dk), F32)
    w2 = w2.at[:GLA_RANK, :n_dk].set(w_gate_f[l]).at[GLA_RANK:2 * GLA_RANK, n_dk:].set(w_gate_b[l])
    b2 = jnp.concatenate([b_gate_f[l], b_gate_b[l]])[None, :]
    wbr1 = w_br[l, 1].reshape(SWA_KV_HEADS, SWA_GROUP, HEAD_DIM, D_MODEL).transpose(1, 0, 2, 3).reshape(512, D_MODEL)
    return {
        "w_uv": w[:, _OFFS[0]:_OFFS[2]].astype(BF16),
        "w_b": jnp.concatenate([wq, col(3), col(4)], axis=1).astype(BF16),
        "w_c": w[:, _OFFS[5]:_OFFS[9]].astype(BF16),
        "w_z": w_z.astype(BF16),
        "w_g": col(11).astype(BF16),
        "w2": w2.astype(BF16),
        "b2": b2,
        "sgu_norm": sgu_norm[l][None, :],
        "w_sgu": w_sgu[l].astype(BF16),
        "b_sgu": jnp.broadcast_to(b_sgu[l][:, :, None], (SGU_GROUPS, SGU_CHUNK, LANES)),
        "q_norm": jnp.tile(q_norm[l], 2)[None, :],
        "k_norm": jnp.tile(k_norm[l], 2)[None, :],
        "gla_norm": gla_norm[l][None, :],
        "w_br": jnp.stack([w_br[l, 0], wbr1, w_br[l, 2]]).astype(BF16),
        "w_o": w_o[l].astype(BF16),
        "w_ff1": w_ff1[l].astype(BF16),
        "w_ff2": w_ff2[l].astype(BF16),
    }


def kernel(x, c, ctx, c_ctx, w_ada, b_ada, w_in, q_norm, k_norm, sink, sgu_norm, w_sgu, b_sgu,
           w_gate_f, b_gate_f, w_gate_b, b_gate_b, gla_norm, w_br, w_o, w_ff1, w_ff2):
    B, N, _ = x.shape
    C = ctx.shape[1]
    depth = w_ada.shape[0]
    tm_lat, tm_ctx = 1024, C
    gla_tile = 512

    cond = jnp.zeros((16, D_MODEL), F32).at[:B].set(c).at[B].set(c_ctx)
    mods = _adaln(cond, w_ada, b_ada)
    rope_lat = _rope_tables(N)
    rope_ctx = _identity_tables(C)
    zero_state = jnp.zeros((B,) + GLA_STATE_SHAPE, F32)

    for l in range(depth):
        lw = _layer_weights(l, w_in, q_norm, k_norm, sgu_norm, w_sgu, b_sgu, w_gate_f, b_gate_f,
                            w_gate_b, b_gate_b, gla_norm, w_br, w_o, w_ff1, w_ff2)
        m = mods[l].reshape(16, 6, D_MODEL)
        lat = [m[:B, j][:, None, :] for j in range(6)]
        cm = [jnp.broadcast_to(m[B, j][None, None, :], (B, 1, D_MODEL)) for j in range(6)]

        (ca, cq, ck, cv, ckq, cvc, cr, cg2) = _proj(ctx, cm[0], cm[1], rope_ctx, lw, tm_ctx)
        oc_f, oc_b, s_ctx = _gla(ckq, cvc, cg2, zero_state, min(gla_tile, C))

        (a, q, k, v, kq, vc, r, g2) = _proj(x, lat[0], lat[1], rope_lat, lw, tm_lat)
        b_out = _attn(q, k, v, ck, cv, sink[l], local=True)
        o_f, o_b, _ = _gla(kq, vc, g2, s_ctx, gla_tile)
        x_new = _merge(x, lat[0], lat[1], lat[2], a, b_out, o_f, o_b, r, lw, tm_lat)
        x_new = _ffn(x_new, lat[3], lat[4], lat[5], lw, tm_lat)

        if l < depth - 1:
            cb_out = _attn(cq, None, None, ck, cv, sink[l], local=False)
            ctx_new = _merge(ctx, cm[0], cm[1], cm[2], ca, cb_out, oc_f, oc_b, cr, lw, tm_ctx)
            ctx = _ffn(ctx_new, cm[3], cm[4], cm[5], lw, tm_ctx)
        x = x_new

    return x
```

```python
import functools
import itertools

import jax
import jax.numpy as jnp
import numpy as np
from jax import lax
from jax.experimental import pallas as pl
from jax.experimental.pallas import tpu as pltpu

D_MODEL = 1024
GRID_W = 64
HEAD_DIM = 64
BRANCH_WIDTH = 512
N_BRANCH = 3
SGU_GROUPS = 4
SGU_CHUNK = 128
SWA_Q_HEADS = 8
SWA_KV_HEADS = 2
SWA_GROUP = 4
SWA_WINDOW = 128
SWA_BLOCK = 128
ROPE_BASE = 10000.0
GLA_HEADS = 4
GLA_DK = 64
GLA_DV = 128
GLA_RANK = 16
GLA_NORMALIZER = 16.0
GLA_CHUNK = 64
FFN_DIM = 4 * D_MODEL
EPS = 1e-6

LANES = 128
VMEM_LIMIT = 56 * 1024 * 1024
NEG_BIG = -1e30
LOG2E = 1.4426950408889634
PROJ_SUBTILE = 256
GLA_CUM_ROWS = 256
ATTN_Q_BLOCKS = 8
GLA_SEQS_PER_STEP = 1
GLA_GROUP_STEPS = 1
GLA_STATE_SHAPE = (2, GLA_HEADS // 2, 2 * GLA_DV, LANES)
BF16 = jnp.bfloat16
F32 = jnp.float32

_SPLITS = (512, 512, 512, 128, 128, 256, 256, 512, 512, 16, 16, 3072)
_OFFS = np.concatenate([[0], np.cumsum(_SPLITS)]).tolist()


def _dot(a, b):
    return jnp.dot(a, b, preferred_element_type=F32)


def _dot_nt(a, b):
    return lax.dot_general(a, b, (((1,), (1,)), ((), ())), preferred_element_type=F32)


def _dot_tn(a, b):
    return lax.dot_general(a, b, (((0,), (0,)), ((), ())), preferred_element_type=F32)


def _const_spec(shape):
    nd = len(shape)
    return pl.BlockSpec(shape, lambda *_: (0,) * nd, pipeline_mode=pl.Buffered(1))


def _params(sem):
    return pltpu.CompilerParams(dimension_semantics=sem, vmem_limit_bytes=VMEM_LIMIT)


def _adaln_kernel(cond_ref, w_ref, b_ref, o_ref):
    cond = cond_ref[...]
    s = (cond * jax.nn.sigmoid(cond)).astype(BF16)
    o_ref[0] = _dot(s, w_ref[0].astype(BF16)) + b_ref[0]


def _adaln(cond, w_ada, b_ada):
    depth = w_ada.shape[0]
    rows = cond.shape[0]
    tn = 1536
    return pl.pallas_call(
        _adaln_kernel,
        grid=(depth, 6 * D_MODEL // tn),
        in_specs=[
            pl.BlockSpec((rows, D_MODEL), lambda l, j: (0, 0)),
            pl.BlockSpec((1, D_MODEL, tn), lambda l, j: (l, 0, j)),
            pl.BlockSpec((1, 1, tn), lambda l, j: (l, 0, j)),
        ],
        out_specs=pl.BlockSpec((1, rows, tn), lambda l, j: (l, 0, j)),
        out_shape=jax.ShapeDtypeStruct((depth, rows, 6 * D_MODEL), F32),
        compiler_params=_params(("arbitrary", "arbitrary")),
        name="adaln",
    )(cond, w_ada, b_ada.reshape(depth, 1, 6 * D_MODEL))


def _sub_tiles(n_rows):
    return [slice(r0, r0 + PROJ_SUBTILE) for r0 in range(0, n_rows, PROJ_SUBTILE)]


def _modulated(x, shift, scale):
    ms = jnp.mean(x * x, axis=-1, keepdims=True)
    return x * lax.rsqrt(ms + EPS) * (1.0 + scale) + shift


def _proj_kernel(x_ref, sh_ref, sc_ref, cos_ref, sa_ref, sb_ref,
                 wuv_ref, wb_ref, wc_ref, wz_ref, w2_ref, b2_ref,
                 sgun_ref, ws_ref, bs_ref, qn_ref, kn_ref,
                 a_ref, q_ref, k_ref, v_ref, kq_ref, vc_ref, r_ref, g2_ref):
    ri = lax.broadcasted_iota(jnp.int32, (2 * LANES, 2 * LANES), 0) // HEAD_DIM
    ci = lax.broadcasted_iota(jnp.int32, (2 * LANES, 2 * LANES), 1) // HEAD_DIM
    group_ones = jnp.where(ri == ci, 1.0, 0.0).astype(BF16)

    tiles = _sub_tiles(x_ref.shape[1])

    hbs = {rows.start: _modulated(x_ref[0, rows], sh_ref[0], sc_ref[0]).astype(BF16) for rows in tiles}
    zs = {0: _dot(hbs[0], wz_ref[...]).astype(BF16)}

    def big(rows):
        hb = hbs[rows.start]
        t = dict(uv=_dot(hb, wuv_ref[...]))
        nxt = rows.stop
        if nxt in hbs:
            zs[nxt] = _dot(hbs[nxt], wz_ref[...]).astype(BF16)
        t["bq"] = _dot(hb, wb_ref[...])
        t["z"] = zs[rows.start]
        return t

    def mixer_c_projection(rows):
        cp = _dot(hbs[rows.start], wc_ref[...])
        n_q = GLA_HEADS * GLA_DK
        kq_ref[0, rows, :n_q] = cp[:, :n_q] * GLA_DK ** -0.5
        kq_ref[0, rows, n_q:] = cp[:, n_q:512]
        vc_ref[0, rows] = cp[:, 512:1024].astype(vc_ref.dtype)
        r_ref[0, rows] = cp[:, 1024:1536].astype(r_ref.dtype)

    def prepare(rows, t):
        bq, uv = t["bq"], t["uv"]
        v_ref[0, rows] = bq[:, 640:768].astype(v_ref.dtype)
        t["sq"] = [(xs * xs).astype(BF16) for xs in (bq[:, :256], bq[:, 256:512], bq[:, 512:768])]
        t["u"] = jax.nn.gelu(uv[:, :BRANCH_WIDTH])
        va = jax.nn.gelu(uv[:, BRANCH_WIDTH:])
        t["vg"] = []
        for g in range(SGU_GROUPS):
            lanes = slice(g * LANES, (g + 1) * LANES)
            vg = va[:, lanes]
            vg = vg * lax.rsqrt(jnp.mean(vg * vg, axis=-1, keepdims=True) + EPS) * sgun_ref[:, lanes]
            t["vg"].append(vg.astype(BF16))

    def small(rows, t):
        bq = t["bq"]
        cos, sa, sb = cos_ref[rows], sa_ref[rows], sb_ref[rows]

        def rope(y):
            return y * cos + pltpu.roll(y, 16, 1) * sa + pltpu.roll(y, LANES - 16, 1) * sb

        logit = _dot(t["z"], w2_ref[...]) + b2_ref[...]
        log_sig = jnp.minimum(logit, 0.0) - jnp.log(1.0 + jnp.exp(-jnp.abs(logit)))
        g2_ref[0, rows] = log_sig * (1.0 / GLA_NORMALIZER)
        inv = [lax.rsqrt(_dot(sq, group_ones) * (1.0 / HEAD_DIM) + EPS) for sq in t["sq"]]
        for s in range(BRANCH_WIDTH // LANES):
            lanes = slice(s * LANES, (s + 1) * LANES)
            qn = bq[:, lanes] * inv[s // 2][:, (s % 2) * LANES:(s % 2 + 1) * LANES] * qn_ref[...]
            q_ref[0, rows, lanes] = (rope(qn) * (HEAD_DIM ** -0.5 * LOG2E)).astype(q_ref.dtype)
        k_ref[0, rows] = rope(bq[:, 512:640] * inv[2][:, :LANES] * kn_ref[...]).astype(k_ref.dtype)
        n_chunk = (rows.stop - rows.start) // SGU_CHUNK
        for g in range(SGU_GROUPS):
            lanes = slice(g * LANES, (g + 1) * LANES)
            vg = jnp.concatenate([t["vg"][g][c * SGU_CHUNK:(c + 1) * SGU_CHUNK] for c in range(n_chunk)], axis=1)
            s_all = _dot(ws_ref[g], vg)
            for c in range(n_chunk):
                cr = slice(c * SGU_CHUNK, (c + 1) * SGU_CHUNK)
                out_rows = slice(rows.start + c * SGU_CHUNK, rows.start + (c + 1) * SGU_CHUNK)
                s = s_all[:, c * LANES:(c + 1) * LANES] + bs_ref[g]
                a_ref[0, out_rows, lanes] = (t["u"][cr, lanes] * s).astype(a_ref.dtype)

    prev = None
    for rows in tiles:
        cur = big(rows)
        if rows is not tiles[-1]:
            mixer_c_projection(rows)
        if prev is not None:
            small(*prev)
        prepare(rows, cur)
        prev = (rows, cur)
    small(*prev)
    mixer_c_projection(tiles[-1])


def _proj(x, shift, scale, rope, lw, tm):
    B, N, _ = x.shape
    cos, sa, sb = rope
    tok = lambda w: pl.BlockSpec((1, tm, w), lambda b, i: (b, i, 0))
    mod = pl.BlockSpec((1, 1, D_MODEL), lambda b, i: (b, 0, 0))
    tab = pl.BlockSpec((tm, LANES), lambda b, i: (i, 0))
    consts = [lw["w_uv"], lw["w_b"], lw["w_c"], lw["w_z"], lw["w2"], lw["b2"],
              lw["sgu_norm"], lw["w_sgu"], lw["b_sgu"], lw["q_norm"], lw["k_norm"]]
    out_w = [(512, BF16), (512, BF16), (128, BF16), (128, BF16), (512, F32), (512, BF16),
             (512, BF16), (512, F32)]
    return pl.pallas_call(
        _proj_kernel,
        grid=(B, N // tm),
        in_specs=[tok(D_MODEL), mod, mod, tab, tab, tab] + [_const_spec(c.shape) for c in consts],
        out_specs=[tok(w) for w, _ in out_w],
        out_shape=[jax.ShapeDtypeStruct((B, N, w), dt) for w, dt in out_w],
        compiler_params=_params(("parallel", "parallel")),
        name="proj",
    )(x, shift, scale, cos, sa, sb, *consts)


def _attn_kernel(sink_ref, q_ref, *refs, local):
    blk = SWA_BLOCK
    n_q = q_ref.shape[1] // blk
    n_slab = BRANCH_WIDTH // LANES
    if local:
        band_ref, kp_ref, kc_ref, kn_ref, vp_ref, vc_ref, vn_ref, kx_ref, vx_ref, o_ref = refs
        k_loc = [kp_ref[0]] + [kc_ref[0, j * blk:(j + 1) * blk] for j in range(n_q)] + [kn_ref[0]]
        v_loc = [vp_ref[0]] + [vc_ref[0, j * blk:(j + 1) * blk] for j in range(n_q)] + [vn_ref[0]]
        keys = [jnp.concatenate(k_loc[j:j + 3] + [kx_ref[0]], axis=0) for j in range(n_q)]
        vals = [jnp.concatenate(v_loc[j:j + 3] + [vx_ref[0]], axis=0) for j in range(n_q)]
        i = pl.program_id(1)
        no_prev = jnp.where(i == 0, NEG_BIG, 0.0)
        no_next = jnp.where(i == pl.num_programs(1) - 1, NEG_BIG, 0.0)
        bias_prev = [band_ref[:, :blk] + no_prev if j == 0 else band_ref[:, :blk] for j in range(n_q)]
        bias_next = [band_ref[:, blk:] + no_next if j == n_q - 1 else band_ref[:, blk:] for j in range(n_q)]
    else:
        kx_ref, vx_ref, o_ref = refs
        keys, vals = [kx_ref[0]] * n_q, [vx_ref[0]] * n_q
    head0_lanes = lax.broadcasted_iota(jnp.int32, (blk, LANES), 1) < HEAD_DIM
    val_lane = lax.broadcasted_iota(jnp.int32, vals[0].shape, 1)

    scores = {}

    def score_block(j):
        rows = slice(j * blk, (j + 1) * blk)
        for h in range(SWA_KV_HEADS):
            own = head0_lanes if h == 0 else ~head0_lanes
            qh = [jnp.where(own, q_ref[0, rows, s * LANES:(s + 1) * LANES], 0.0).astype(BF16)
                  for s in range(n_slab)]
            scores[j, h] = _dot_nt(jnp.concatenate(qh, axis=0), keys[j])

    score_block(0)
    for j in range(n_q):
        if j + 1 < n_q:
            score_block(j + 1)
        outs = {}
        for h in range(SWA_KV_HEADS):
            ps, sink_terms = [], []
            for s in range(n_slab):
                sc = scores[j, h][s * blk:(s + 1) * blk]
                if local:
                    sc = jnp.concatenate([sc[:, :blk] + bias_prev[j], sc[:, blk:2 * blk],
                                          sc[:, 2 * blk:3 * blk] + bias_next[j], sc[:, 3 * blk:]], axis=1)
                sink = sink_ref[h * SWA_GROUP + s] * LOG2E
                m = jnp.maximum(jnp.max(sc, axis=-1, keepdims=True), sink)
                ps.append(jnp.exp2(sc - m).astype(BF16))
                sink_terms.append(jnp.exp2(sink - m))
            own = (val_lane < HEAD_DIM) if h == 0 else (val_lane >= HEAD_DIM)
            pv = _dot(jnp.concatenate(ps, axis=0), jnp.where(own, vals[j], jnp.ones_like(vals[j])))
            for s in range(n_slab):
                pv_s = pv[s * blk:(s + 1) * blk]
                denom = pltpu.roll(pv_s, HEAD_DIM, 1) + sink_terms[s]
                outs[s, h] = pv_s / denom
        for s in range(n_slab):
            o_ref[0, j * blk:(j + 1) * blk, s * LANES:(s + 1) * LANES] = jnp.where(
                head0_lanes, outs[s, 0], outs[s, 1]).astype(o_ref.dtype)


def _band_bias(blk):
    qi = np.arange(blk)[:, None]
    kj = np.arange(blk)[None, :]
    prev_ok = np.abs(kj - blk - qi) <= SWA_WINDOW
    next_ok = np.abs(kj + blk - qi) <= SWA_WINDOW
    return jnp.asarray(np.where(np.concatenate([prev_ok, next_ok], axis=1), 0.0, NEG_BIG), F32)


def _attn(q, k, v, k_ctx, v_ctx, sink, local):
    B, N, _ = q.shape
    C = k_ctx.shape[1]
    blk = SWA_BLOCK
    nb = N // blk
    n_q = min(ATTN_Q_BLOCKS, nb)
    steps = nb // n_q
    qspec = pl.BlockSpec((1, n_q * blk, BRANCH_WIDTH), lambda b, i: (b, i, 0))
    ctx_spec = pl.BlockSpec((1, C, LANES), lambda b, i: (b, 0, 0))
    smem = pl.BlockSpec(memory_space=pltpu.SMEM)
    if local:
        prev = pl.BlockSpec((1, blk, LANES), lambda b, i: (b, jnp.maximum(n_q * i - 1, 0), 0))
        cur = pl.BlockSpec((1, n_q * blk, LANES), lambda b, i: (b, i, 0))
        nxt = pl.BlockSpec((1, blk, LANES), lambda b, i: (b, jnp.minimum(n_q * (i + 1), nb - 1), 0))
        band = _band_bias(blk)
        in_specs = [smem, qspec, _const_spec(band.shape), prev, cur, nxt, prev, cur, nxt, ctx_spec, ctx_spec]
        args = (sink, q, band, k, k, k, v, v, v, k_ctx, v_ctx)
    else:
        in_specs = [smem, qspec, ctx_spec, ctx_spec]
        args = (sink, q, k_ctx, v_ctx)
    return pl.pallas_call(
        functools.partial(_attn_kernel, local=local),
        grid=(B, steps),
        in_specs=in_specs,
        out_specs=qspec,
        out_shape=jax.ShapeDtypeStruct((B, N, BRANCH_WIDTH), BF16),
        compiler_params=_params(("parallel", "parallel")),
        name="attn_local" if local else "attn_ctx",
    )(*args)


def _gla_kernel(kqf_ref, vf_ref, gf_ref, kqb_ref, vb_ref, gb_ref, s0_ref,
                of_ref, ob_ref, sfin_ref, st_ref):
    i = pl.program_id(1)
    T = kqf_ref.shape[1]
    n_chunk = T // GLA_CHUNK

    n_seq = kqf_ref.shape[0]

    @pl.when(i == 0)
    def _():
        st_ref[...] = s0_ref[...]

    cum_rows = min(T, GLA_CUM_ROWS)
    ri = lax.broadcasted_iota(jnp.int32, (cum_rows, cum_rows), 0)
    ci = lax.broadcasted_iota(jnp.int32, (cum_rows, cum_rows), 1)
    same_chunk = (ri // GLA_CHUNK) == (ci // GLA_CHUNK)
    li = lax.broadcasted_iota(jnp.int32, (GLA_CHUNK, 2 * GLA_CHUNK), 0)
    mi = lax.broadcasted_iota(jnp.int32, (GLA_CHUNK, 2 * GLA_CHUNK), 1) % GLA_CHUNK
    first_head_lanes = lax.broadcasted_iota(jnp.int32, (GLA_CHUNK, LANES), 1) < GLA_DK
    own_block = ((lax.broadcasted_iota(jnp.int32, (2 * GLA_DV, LANES), 0) < GLA_DV)
                 == (lax.broadcasted_iota(jnp.int32, (2 * GLA_DV, LANES), 1) < GLA_DK))

    dirs = ((kqf_ref, vf_ref, gf_ref, of_ref), (kqb_ref, vb_ref, gb_ref, ob_ref))
    cums = {}
    for d, (kq_ref, v_ref, g_ref, o_ref) in enumerate(dirs):
        tri_full = jnp.where(same_chunk & ((ci >= ri) if d == 1 else (ci <= ri)), 1.0, 0.0).astype(BF16)
        for b in range(n_seq):
            cums[b, d] = jnp.concatenate([_split_dot_left(tri_full, g_ref[b, r0:r0 + cum_rows])
                                          for r0 in range(0, T, cum_rows)], axis=0)

    def chunk_of(d, step):
        return n_chunk - 1 - step if d == 1 else step

    def operands(b, d, p, c, anchor):
        kq_ref = dirs[d][0]
        lanes = slice(p * LANES, (p + 1) * LANES)
        k_lanes = slice(GLA_HEADS * GLA_DK + p * LANES, GLA_HEADS * GLA_DK + (p + 1) * LANES)
        rows = slice(c * GLA_CHUNK, (c + 1) * GLA_CHUNK)
        ref_row = GLA_CHUNK // 2 - 1 if d == 1 else GLA_CHUNK // 2
        tot_row = 0 if d == 1 else GLA_CHUNK - 1
        bc = cums[b, d][rows, lanes]
        if anchor is not None:
            bc = bc + anchor
        q2 = kq_ref[b, rows, lanes]
        k2 = kq_ref[b, rows, k_lanes]
        b_ref = bc[ref_row:ref_row + 1]
        b_tot = bc[tot_row:tot_row + 1]
        ke = k2 * jnp.exp(b_ref - bc)
        return dict(
            qe=(q2 * jnp.exp(bc - b_ref)).astype(BF16),
            q_in=(q2 * jnp.exp(bc)).astype(BF16),
            k_out=(k2 * jnp.exp(b_tot - bc)).astype(BF16),
            decay=jnp.exp(b_tot),
            ke=jnp.concatenate([jnp.where(first_head_lanes, ke, 0.0),
                                jnp.where(first_head_lanes, 0.0, ke)], axis=0).astype(BF16))

    ops = {}
    anchors = {}

    def score_group(gi, steps):
        anchor = anchors.get(gi - 2)
        for step in steps:
            for b in range(n_seq):
                for d in range(2):
                    for p in range(GLA_HEADS // 2):
                        u = operands(b, d, p, chunk_of(d, step), anchor)
                        keep = (mi >= li) if d == 1 else (mi <= li)
                        scores = _dot_nt(u["qe"], u["ke"])
                        u["pm"] = jnp.where(keep, scores, 0.0).astype(BF16)
                        ops[b, d, p, step] = u
        anchors[gi] = jnp.minimum(jnp.abs(scores[:1]), 0.0)

    zeros_v = jnp.zeros((GLA_CHUNK, GLA_DV), BF16)

    def scan_group(steps):
        for step, b, (d, (kq_ref, v_ref, g_ref, o_ref)), p in itertools.product(
                steps, range(n_seq), enumerate(dirs), range(GLA_HEADS // 2)):
            c = chunk_of(d, step)
            rows = slice(c * GLA_CHUNK, (c + 1) * GLA_CHUNK)
            lanes = slice(p * 2 * GLA_DV, (p + 1) * 2 * GLA_DV)
            u = ops.pop((b, d, p, step))
            v2 = v_ref[b, rows, lanes]
            v_diag = jnp.concatenate(
                [jnp.concatenate([v2[:, :GLA_DV], zeros_v], axis=1),
                 jnp.concatenate([zeros_v, v2[:, GLA_DV:]], axis=1)], axis=0)
            st = st_ref[b, d, p]
            o = _dot(u["pm"], v_diag) + _dot_nt(u["q_in"], st.astype(BF16))
            o_ref[b, rows, lanes] = o.astype(o_ref.dtype)
            st_ref[b, d, p] = st * u["decay"] + jnp.where(own_block, _dot_tn(v2, u["k_out"]), 0.0)

    groups = [range(s, min(s + GLA_GROUP_STEPS, n_chunk)) for s in range(0, n_chunk, GLA_GROUP_STEPS)]
    score_group(0, groups[0])
    for gi, steps in enumerate(groups):
        if gi + 1 < len(groups):
            score_group(gi + 1, groups[gi + 1])
        scan_group(steps)

    @pl.when(i == pl.num_programs(1) - 1)
    def _():
        sfin_ref[...] = st_ref[...]


def _split_dot_left(m, a):
    hi = a.astype(BF16)
    lo = (a - hi.astype(F32)).astype(BF16)
    return _dot(m, hi) + _dot(m, lo)


def _gla(kq, v, g2, s0, T):
    B, N, _ = kq.shape
    nt = N // T
    nb = GLA_SEQS_PER_STEP
    fwd = lambda w: pl.BlockSpec((nb, T, w), lambda b, i: (b, i, 0))
    bwd = lambda w: pl.BlockSpec((nb, T, w), lambda b, i: (b, nt - 1 - i, 0))
    g_fwd = pl.BlockSpec((nb, T, 256), lambda b, i: (b, i, 0))
    g_bwd = pl.BlockSpec((nb, T, 256), lambda b, i: (b, nt - 1 - i, 1))
    st_shape = GLA_STATE_SHAPE
    st_spec = pl.BlockSpec((nb,) + st_shape, lambda b, i: (b, 0, 0, 0, 0))
    return pl.pallas_call(
        _gla_kernel,
        grid=(B // nb, nt),
        in_specs=[fwd(512), fwd(512), g_fwd, bwd(512), bwd(512), g_bwd, st_spec],
        out_specs=[fwd(512), bwd(512), st_spec],
        out_shape=[jax.ShapeDtypeStruct((B, N, 512), BF16), jax.ShapeDtypeStruct((B, N, 512), BF16),
                   jax.ShapeDtypeStruct((B,) + st_shape, F32)],
        scratch_shapes=[pltpu.VMEM((nb,) + st_shape, F32)],
        compiler_params=_params(("parallel", "arbitrary")),
        name="gla",
    )(kq, v, g2, kq, v, g2, s0)


def _merge_kernel(x_ref, sh_ref, sc_ref, g1_ref, a_ref, b_ref, of_ref, ob_ref, r_ref,
                  gn_ref, wg_ref, wbr_ref, wo_ref, o_ref):
    def gated_mix(rows):
        hb = _modulated(x_ref[0, rows], sh_ref[0], sc_ref[0]).astype(BF16)
        o = of_ref[0, rows].astype(F32) + ob_ref[0, rows].astype(F32)
        cs = []
        for h in range(GLA_HEADS):
            oh = o[:, h * LANES:(h + 1) * LANES]
            cs.append(oh * lax.rsqrt(jnp.mean(oh * oh, axis=-1, keepdims=True) + EPS) * gn_ref[...])
        r = r_ref[0, rows].astype(F32)
        c_out = (jnp.concatenate(cs, axis=1) * (r * jax.nn.sigmoid(r))).astype(BF16)
        branches = (a_ref[0, rows], b_ref[0, rows], c_out)
        mixed = None
        for k in range(N_BRANCH):
            gate = jax.nn.sigmoid(_dot(hb, wg_ref[:, k * D_MODEL:(k + 1) * D_MODEL])).astype(BF16)
            term = gate.astype(F32) * _dot(branches[k], wbr_ref[k])
            mixed = term if mixed is None else mixed + term
        return mixed.astype(BF16)

    def project_out(rows, mixed):
        o_ref[0, rows] = x_ref[0, rows] + g1_ref[0] * _dot(mixed, wo_ref[...])

    prev = None
    for rows in _sub_tiles(x_ref.shape[1]):
        mixed = gated_mix(rows)
        if prev is not None:
            project_out(*prev)
        prev = (rows, mixed)
    project_out(*prev)


def _merge(x, shift, scale, g1, a, b, o_f, o_b, r, lw, tm):
    B, N, _ = x.shape
    tok = lambda w: pl.BlockSpec((1, tm, w), lambda b_, i: (b_, i, 0))
    mod = pl.BlockSpec((1, 1, D_MODEL), lambda b_, i: (b_, 0, 0))
    consts = [lw["gla_norm"], lw["w_g"], lw["w_br"], lw["w_o"]]
    return pl.pallas_call(
        _merge_kernel,
        grid=(B, N // tm),
        in_specs=[tok(D_MODEL), mod, mod, mod, tok(512), tok(512), tok(512), tok(512), tok(512)]
                 + [_const_spec(c.shape) for c in consts],
        out_specs=tok(D_MODEL),
        out_shape=jax.ShapeDtypeStruct((B, N, D_MODEL), F32),
        compiler_params=_params(("parallel", "parallel")),
        name="merge",
    )(x, shift, scale, g1, a, b, o_f, o_b, r, *consts)


def _ffn_kernel(x_ref, sh_ref, sc_ref, g2_ref, w1_ref, w2_ref, o_ref):
    x = x_ref[0]
    hb = _modulated(x, sh_ref[0], sc_ref[0]).astype(BF16)
    t = jnp.maximum(_dot(hb, w1_ref[...]), 0.0)
    f = _dot((t * t).astype(BF16), w2_ref[...])
    o_ref[0] = x + g2_ref[0] * f


def _ffn(x, shift, scale, gate, lw, tm):
    B, N, _ = x.shape
    tok = pl.BlockSpec((1, tm, D_MODEL), lambda b, i: (b, i, 0))
    mod = pl.BlockSpec((1, 1, D_MODEL), lambda b, i: (b, 0, 0))
    return pl.pallas_call(
        _ffn_kernel,
        grid=(B, N // tm),
        in_specs=[tok, mod, mod, mod, _const_spec(lw["w_ff1"].shape), _const_spec(lw["w_ff2"].shape)],
        out_specs=tok,
        out_shape=jax.ShapeDtypeStruct((B, N, D_MODEL), F32),
        compiler_params=_params(("parallel", "parallel")),
        name="ffn",
    )(x, shift, scale, gate, lw["w_ff1"], lw["w_ff2"])


def _rope_tables(n_tokens):
    lane = np.arange(LANES)
    d = lane % HEAD_DIM
    use_col = (d // (HEAD_DIM // 2)) == 1
    second = ((d % (HEAD_DIM // 2)) // (HEAD_DIM // 4)) == 1
    f = d % (HEAD_DIM // 4)
    n_freq = HEAD_DIM // 4
    freqs = ROPE_BASE ** (-jnp.arange(n_freq, dtype=F32) / n_freq)
    t = jnp.arange(n_tokens)
    row = (t // GRID_W).astype(F32)
    col = (t % GRID_W).astype(F32)
    ang_row = row[:, None] * freqs[None, :]
    ang_col = col[:, None] * freqs[None, :]
    ang = jnp.where(jnp.asarray(use_col)[None, :], ang_col[:, f], ang_row[:, f])
    cos, sin = jnp.cos(ang), jnp.sin(ang)
    second = jnp.asarray(second)[None, :]
    return cos, jnp.where(second, sin, 0.0), jnp.where(second, 0.0, -sin)


def _identity_tables(n_tokens):
    return (jnp.ones((n_tokens, LANES), F32), jnp.zeros((n_tokens, LANES), F32),
            jnp.zeros((n_tokens, LANES), F32))


def _layer_weights(l, w_in, q_norm, k_norm, sgu_norm, w_sgu, b_sgu, w_gate_f, b_gate_f,
                   w_gate_b, b_gate_b, gla_norm, w_br, w_o, w_ff1, w_ff2):
    w = w_in[l]
    col = lambda i: w[:, _OFFS[i]:_OFFS[i + 1]]
    wq = col(2).reshape(D_MODEL, SWA_KV_HEADS, SWA_GROUP, HEAD_DIM).transpose(0, 2, 1, 3).reshape(D_MODEL, 512)
    w_z = jnp.zeros((D_MODEL, LANES), F32).at[:, :2 * GLA_RANK].set(w[:, _OFFS[9]:_OFFS[11]])
    n_dk = GLA_HEADS * GLA_DK
    w2 = jnp.zeros((LANES, 2 * n_dk), F32)
    w2 = w2.at[:GLA_RANK, :n_dk].set(w_gate_f[l]).at[GLA_RANK:2 * GLA_RANK, n_dk:].set(w_gate_b[l])
    b2 = jnp.concatenate([b_gate_f[l], b_gate_b[l]])[None, :]
    wbr1 = w_br[l, 1].reshape(SWA_KV_HEADS, SWA_GROUP, HEAD_DIM, D_MODEL).transpose(1, 0, 2, 3).reshape(512, D_MODEL)
    return {
        "w_uv": w[:, _OFFS[0]:_OFFS[2]].astype(BF16),
        "w_b": jnp.concatenate([wq, col(3), col(4)], axis=1).astype(BF16),
        "w_c": w[:, _OFFS[5]:_OFFS[9]].astype(BF16),
        "w_z": w_z.astype(BF16),
        "w_g": col(11).astype(BF16),
        "w2": w2.astype(BF16),
        "b2": b2,
        "sgu_norm": sgu_norm[l][None, :],
        "w_sgu": w_sgu[l].astype(BF16),
        "b_sgu": jnp.broadcast_to(b_sgu[l][:, :, None], (SGU_GROUPS, SGU_CHUNK, LANES)),
        "q_norm": jnp.tile(q_norm[l], 2)[None, :],
        "k_norm": jnp.tile(k_norm[l], 2)[None, :],
        "gla_norm": gla_norm[l][None, :],
        "w_br": jnp.stack([w_br[l, 0], wbr1, w_br[l, 2]]).astype(BF16),
        "w_o": w_o[l].astype(BF16),
        "w_ff1": w_ff1[l].astype(BF16),
        "w_ff2": w_ff2[l].astype(BF16),
    }


def kernel(x, c, ctx, c_ctx, w_ada, b_ada, w_in, q_norm, k_norm, sink, sgu_norm, w_sgu, b_sgu,
           w_gate_f, b_gate_f, w_gate_b, b_gate_b, gla_norm, w_br, w_o, w_ff1, w_ff2):
    B, N, _ = x.shape
    C = ctx.shape[1]
    depth = w_ada.shape[0]
    tm_lat, tm_ctx = 1024, C
    gla_tile = 1024

    cond = jnp.zeros((16, D_MODEL), F32).at[:B].set(c).at[B].set(c_ctx)
    mods = _adaln(cond, w_ada, b_ada)
    rope_lat = _rope_tables(N)
    rope_ctx = _identity_tables(C)
    zero_state = jnp.zeros((B,) + GLA_STATE_SHAPE, F32)

    for l in range(depth):
        lw = _layer_weights(l, w_in, q_norm, k_norm, sgu_norm, w_sgu, b_sgu, w_gate_f, b_gate_f,
                            w_gate_b, b_gate_b, gla_norm, w_br, w_o, w_ff1, w_ff2)
        m = mods[l].reshape(16, 6, D_MODEL)
        lat = [m[:B, j][:, None, :] for j in range(6)]
        cm = [jnp.broadcast_to(m[B, j][None, None, :], (B, 1, D_MODEL)) for j in range(6)]

        (ca, cq, ck, cv, ckq, cvc, cr, cg2) = _proj(ctx, cm[0], cm[1], rope_ctx, lw, tm_ctx)
        oc_f, oc_b, s_ctx = _gla(ckq, cvc, cg2, zero_state, min(gla_tile, C))

        (a, q, k, v, kq, vc, r, g2) = _proj(x, lat[0], lat[1], rope_lat, lw, tm_lat)
        b_out = _attn(q, k, v, ck, cv, sink[l], local=True)
        o_f, o_b, _ = _gla(kq, vc, g2, s_ctx, gla_tile)
        x_new = _merge(x, lat[0], lat[1], lat[2], a, b_out, o_f, o_b, r, lw, tm_lat)
        x_new = _ffn(x_new, lat[3], lat[4], lat[5], lw, tm_lat)

        if l < depth - 1:
            cb_out = _attn(cq, None, None, ck, cv, sink[l], local=False)
            ctx_new = _merge(ctx, cm[0], cm[1], cm[2], ca, cb_out, oc_f, oc_b, cr, lw, tm_ctx)
            ctx = _ffn(ctx_new, cm[3], cm[4], cm[5], lw, tm_ctx)
        x = x_new

    return x
```

```python
import functools
import itertools

import jax
import jax.numpy as jnp
import numpy as np
from jax import lax
from jax.experimental import pallas as pl
from jax.experimental.pallas import tpu as pltpu

D_MODEL = 1024
GRID_W = 64
HEAD_DIM = 64
BRANCH_WIDTH = 512
N_BRANCH = 3
SGU_GROUPS = 4
SGU_CHUNK = 128
SWA_Q_HEADS = 8
SWA_KV_HEADS = 2
SWA_GROUP = 4
SWA_WINDOW = 128
SWA_BLOCK = 128
ROPE_BASE = 10000.0
GLA_HEADS = 4
GLA_DK = 64
GLA_DV = 128
GLA_RANK = 16
GLA_NORMALIZER = 16.0
GLA_CHUNK = 64
FFN_DIM = 4 * D_MODEL
EPS = 1e-6

LANES = 128
VMEM_LIMIT = 56 * 1024 * 1024
NEG_BIG = -1e30
LOG2E = 1.4426950408889634
PROJ_SUBTILE = 256
GLA_CUM_ROWS = 256
ATTN_Q_BLOCKS = 8
GLA_SEQS_PER_STEP = 1
GLA_GROUP_STEPS = 1
GLA_STATE_SHAPE = (2, GLA_HEADS // 2, 2 * GLA_DV, LANES)
BF16 = jnp.bfloat16
F32 = jnp.float32

_SPLITS = (512, 512, 512, 128, 128, 256, 256, 512, 512, 16, 16, 3072)
_OFFS = np.concatenate([[0], np.cumsum(_SPLITS)]).tolist()


def _dot(a, b):
    return jnp.dot(a, b, preferred_element_type=F32)


def _dot_nt(a, b):
    return lax.dot_general(a, b, (((1,), (1,)), ((), ())), preferred_element_type=F32)


def _dot_tn(a, b):
    return lax.dot_general(a, b, (((0,), (0,)), ((), ())), preferred_element_type=F32)


def _const_spec(shape):
    nd = len(shape)
    return pl.BlockSpec(shape, lambda *_: (0,) * nd, pipeline_mode=pl.Buffered(1))


def _params(sem):
    return pltpu.CompilerParams(dimension_semantics=sem, vmem_limit_bytes=VMEM_LIMIT)


def _adaln_kernel(cond_ref, w_ref, b_ref, o_ref):
    cond = cond_ref[...]
    s = (cond * jax.nn.sigmoid(cond)).astype(BF16)
    o_ref[0] = _dot(s, w_ref[0].astype(BF16)) + b_ref[0]


def _adaln(cond, w_ada, b_ada):
    depth = w_ada.shape[0]
    rows = cond.shape[0]
    tn = 3072
    return pl.pallas_call(
        _adaln_kernel,
        grid=(depth, 6 * D_MODEL // tn),
        in_specs=[
            pl.BlockSpec((rows, D_MODEL), lambda l, j: (0, 0)),
            pl.BlockSpec((1, D_MODEL, tn), lambda l, j: (l, 0, j)),
            pl.BlockSpec((1, 1, tn), lambda l, j: (l, 0, j)),
        ],
        out_specs=pl.BlockSpec((1, rows, tn), lambda l, j: (l, 0, j)),
        out_shape=jax.ShapeDtypeStruct((depth, rows, 6 * D_MODEL), F32),
        compiler_params=_params(("arbitrary", "arbitrary")),
        name="adaln",
    )(cond, w_ada, b_ada.reshape(depth, 1, 6 * D_MODEL))


def _sub_tiles(n_rows):
    return [slice(r0, r0 + PROJ_SUBTILE) for r0 in range(0, n_rows, PROJ_SUBTILE)]


def _modulated(x, shift, scale):
    ms = jnp.mean(x * x, axis=-1, keepdims=True)
    return x * lax.rsqrt(ms + EPS) * (1.0 + scale) + shift


def _proj_kernel(x_ref, sh_ref, sc_ref, cos_ref, sa_ref, sb_ref,
                 wuv_ref, wb_ref, wc_ref, wz_ref, w2_ref, b2_ref,
                 sgun_ref, ws_ref, bs_ref, qn_ref, kn_ref,
                 a_ref, q_ref, k_ref, v_ref, kq_ref, vc_ref, r_ref, g2_ref):
    ri = lax.broadcasted_iota(jnp.int32, (2 * LANES, 2 * LANES), 0) // HEAD_DIM
    ci = lax.broadcasted_iota(jnp.int32, (2 * LANES, 2 * LANES), 1) // HEAD_DIM
    group_ones = jnp.where(ri == ci, 1.0, 0.0).astype(BF16)

    tiles = _sub_tiles(x_ref.shape[1])

    hbs = {rows.start: _modulated(x_ref[0, rows], sh_ref[0], sc_ref[0]).astype(BF16) for rows in tiles}
    zs = {0: _dot(hbs[0], wz_ref[...]).astype(BF16)}

    def big(rows):
        hb = hbs[rows.start]
        t = dict(uv=_dot(hb, wuv_ref[...]))
        nxt = rows.stop
        if nxt in hbs:
            zs[nxt] = _dot(hbs[nxt], wz_ref[...]).astype(BF16)
        t["bq"] = _dot(hb, wb_ref[...])
        t["z"] = zs[rows.start]
        return t

    def mixer_c_projection(rows):
        cp = _dot(hbs[rows.start], wc_ref[...])
        n_q = GLA_HEADS * GLA_DK
        kq_ref[0, rows, :n_q] = cp[:, :n_q] * GLA_DK ** -0.5
        kq_ref[0, rows, n_q:] = cp[:, n_q:512]
        vc_ref[0, rows] = cp[:, 512:1024].astype(vc_ref.dtype)
        r_ref[0, rows] = cp[:, 1024:1536].astype(r_ref.dtype)

    def prepare(rows, t):
        bq, uv = t["bq"], t["uv"]
        v_ref[0, rows] = bq[:, 640:768].astype(v_ref.dtype)
        t["sq"] = [(xs * xs).astype(BF16) for xs in (bq[:, :256], bq[:, 256:512], bq[:, 512:768])]
        t["u"] = jax.nn.gelu(uv[:, :BRANCH_WIDTH])
        va = jax.nn.gelu(uv[:, BRANCH_WIDTH:])
        t["vg"] = []
        for g in range(SGU_GROUPS):
            lanes = slice(g * LANES, (g + 1) * LANES)
            vg = va[:, lanes]
            vg = vg * lax.rsqrt(jnp.mean(vg * vg, axis=-1, keepdims=True) + EPS) * sgun_ref[:, lanes]
            t["vg"].append(vg.astype(BF16))

    def small(rows, t):
        bq = t["bq"]
        cos, sa, sb = cos_ref[rows], sa_ref[rows], sb_ref[rows]

        def rope(y):
            return y * cos + pltpu.roll(y, 16, 1) * sa + pltpu.roll(y, LANES - 16, 1) * sb

        logit = _dot(t["z"], w2_ref[...]) + b2_ref[...]
        log_sig = jnp.minimum(logit, 0.0) - jnp.log(1.0 + jnp.exp(-jnp.abs(logit)))
        g2_ref[0, rows] = log_sig * (1.0 / GLA_NORMALIZER)
        inv = [lax.rsqrt(_dot(sq, group_ones) * (1.0 / HEAD_DIM) + EPS) for sq in t["sq"]]
        for s in range(BRANCH_WIDTH // LANES):
            lanes = slice(s * LANES, (s + 1) * LANES)
            qn = bq[:, lanes] * inv[s // 2][:, (s % 2) * LANES:(s % 2 + 1) * LANES] * qn_ref[...]
            q_ref[0, rows, lanes] = (rope(qn) * (HEAD_DIM ** -0.5 * LOG2E)).astype(q_ref.dtype)
        k_ref[0, rows] = rope(bq[:, 512:640] * inv[2][:, :LANES] * kn_ref[...]).astype(k_ref.dtype)
        n_chunk = (rows.stop - rows.start) // SGU_CHUNK
        for g in range(SGU_GROUPS):
            lanes = slice(g * LANES, (g + 1) * LANES)
            vg = jnp.concatenate([t["vg"][g][c * SGU_CHUNK:(c + 1) * SGU_CHUNK] for c in range(n_chunk)], axis=1)
            s_all = _dot(ws_ref[g], vg)
            for c in range(n_chunk):
                cr = slice(c * SGU_CHUNK, (c + 1) * SGU_CHUNK)
                out_rows = slice(rows.start + c * SGU_CHUNK, rows.start + (c + 1) * SGU_CHUNK)
                s = s_all[:, c * LANES:(c + 1) * LANES] + bs_ref[g]
                a_ref[0, out_rows, lanes] = (t["u"][cr, lanes] * s).astype(a_ref.dtype)

    prev = None
    for rows in tiles:
        cur = big(rows)
        if rows is not tiles[-1]:
            mixer_c_projection(rows)
        if prev is not None:
            small(*prev)
        prepare(rows, cur)
        prev = (rows, cur)
    small(*prev)
    mixer_c_projection(tiles[-1])


def _mod_spec(j, mod_row):
    if mod_row is None:
        return pl.BlockSpec((1, 1, D_MODEL), lambda b, i: (b * 6 + j, 0, 0))
    return pl.BlockSpec((1, 1, D_MODEL), lambda b, i: (mod_row * 6 + j, 0, 0))


def _proj(x, mods, mod_row, rope, lw, tm):
    B, N, _ = x.shape
    cos, sa, sb = rope
    tok = lambda w: pl.BlockSpec((1, tm, w), lambda b, i: (b, i, 0))
    tab = pl.BlockSpec((tm, LANES), lambda b, i: (i, 0))
    consts = [lw["w_uv"], lw["w_b"], lw["w_c"], lw["w_z"], lw["w2"], lw["b2"],
              lw["sgu_norm"], lw["w_sgu"], lw["b_sgu"], lw["q_norm"], lw["k_norm"]]
    out_w = [(512, BF16), (512, BF16), (128, BF16), (128, BF16), (512, F32), (512, BF16),
             (512, BF16), (512, F32)]
    return pl.pallas_call(
        _proj_kernel,
        grid=(B, N // tm),
        in_specs=[tok(D_MODEL), _mod_spec(0, mod_row), _mod_spec(1, mod_row), tab, tab, tab]
                 + [_const_spec(c.shape) for c in consts],
        out_specs=[tok(w) for w, _ in out_w],
        out_shape=[jax.ShapeDtypeStruct((B, N, w), dt) for w, dt in out_w],
        compiler_params=_params(("parallel", "parallel")),
        name="proj",
    )(x, mods, mods, cos, sa, sb, *consts)


def _attn_kernel(sink_ref, q_ref, *refs, local):
    blk = SWA_BLOCK
    n_q = q_ref.shape[1] // blk
    n_slab = BRANCH_WIDTH // LANES
    if local:
        band_ref, kp_ref, kc_ref, kn_ref, vp_ref, vc_ref, vn_ref, kx_ref, vx_ref, o_ref = refs
        k_loc = [kp_ref[0]] + [kc_ref[0, j * blk:(j + 1) * blk] for j in range(n_q)] + [kn_ref[0]]
        v_loc = [vp_ref[0]] + [vc_ref[0, j * blk:(j + 1) * blk] for j in range(n_q)] + [vn_ref[0]]
        keys = [jnp.concatenate(k_loc[j:j + 3] + [kx_ref[0]], axis=0) for j in range(n_q)]
        vals = [jnp.concatenate(v_loc[j:j + 3] + [vx_ref[0]], axis=0) for j in range(n_q)]
        i = pl.program_id(1)
        no_prev = jnp.where(i == 0, NEG_BIG, 0.0)
        no_next = jnp.where(i == pl.num_programs(1) - 1, NEG_BIG, 0.0)
        bias_prev = [band_ref[:, :blk] + no_prev if j == 0 else band_ref[:, :blk] for j in range(n_q)]
        bias_next = [band_ref[:, blk:] + no_next if j == n_q - 1 else band_ref[:, blk:] for j in range(n_q)]
    else:
        kx_ref, vx_ref, o_ref = refs
        keys, vals = [kx_ref[0]] * n_q, [vx_ref[0]] * n_q
    head0_lanes = lax.broadcasted_iota(jnp.int32, (blk, LANES), 1) < HEAD_DIM
    val_lane = lax.broadcasted_iota(jnp.int32, vals[0].shape, 1)

    scores = {}

    def score_block(j):
        rows = slice(j * blk, (j + 1) * blk)
        for h in range(SWA_KV_HEADS):
            own = head0_lanes if h == 0 else ~head0_lanes
            qh = [jnp.where(own, q_ref[0, rows, s * LANES:(s + 1) * LANES], 0.0).astype(BF16)
                  for s in range(n_slab)]
            scores[j, h] = _dot_nt(jnp.concatenate(qh, axis=0), keys[j])

    score_block(0)
    for j in range(n_q):
        if j + 1 < n_q:
            score_block(j + 1)
        outs = {}
        for h in range(SWA_KV_HEADS):
            ps, sink_terms = [], []
            for s in range(n_slab):
                sc = scores[j, h][s * blk:(s + 1) * blk]
                if local:
                    sc = jnp.concatenate([sc[:, :blk] + bias_prev[j], sc[:, blk:2 * blk],
                                          sc[:, 2 * blk:3 * blk] + bias_next[j], sc[:, 3 * blk:]], axis=1)
                sink = sink_ref[h * SWA_GROUP + s] * LOG2E
                m = jnp.maximum(jnp.max(sc, axis=-1, keepdims=True), sink)
                ps.append(jnp.exp2(sc - m).astype(BF16))
                sink_terms.append(jnp.exp2(sink - m))
            own = (val_lane < HEAD_DIM) if h == 0 else (val_lane >= HEAD_DIM)
            pv = _dot(jnp.concatenate(ps, axis=0), jnp.where(own, vals[j], jnp.ones_like(vals[j])))
            for s in range(n_slab):
                pv_s = pv[s * blk:(s + 1) * blk]
                denom = pltpu.roll(pv_s, HEAD_DIM, 1) + sink_terms[s]
                outs[s, h] = pv_s / denom
        for s in range(n_slab):
            o_ref[0, j * blk:(j + 1) * blk, s * LANES:(s + 1) * LANES] = jnp.where(
                head0_lanes, outs[s, 0], outs[s, 1]).astype(o_ref.dtype)


def _band_bias(blk):
    qi = np.arange(blk)[:, None]
    kj = np.arange(blk)[None, :]
    prev_ok = np.abs(kj - blk - qi) <= SWA_WINDOW
    next_ok = np.abs(kj + blk - qi) <= SWA_WINDOW
    return jnp.asarray(np.where(np.concatenate([prev_ok, next_ok], axis=1), 0.0, NEG_BIG), F32)


def _attn(q, k, v, k_ctx, v_ctx, sink, local):
    B, N, _ = q.shape
    C = k_ctx.shape[1]
    blk = SWA_BLOCK
    nb = N // blk
    n_q = min(ATTN_Q_BLOCKS, nb)
    steps = nb // n_q
    qspec = pl.BlockSpec((1, n_q * blk, BRANCH_WIDTH), lambda b, i: (b, i, 0))
    ctx_spec = pl.BlockSpec((1, C, LANES), lambda b, i: (b, 0, 0))
    smem = pl.BlockSpec(memory_space=pltpu.SMEM)
    if local:
        prev = pl.BlockSpec((1, blk, LANES), lambda b, i: (b, jnp.maximum(n_q * i - 1, 0), 0))
        cur = pl.BlockSpec((1, n_q * blk, LANES), lambda b, i: (b, i, 0))
        nxt = pl.BlockSpec((1, blk, LANES), lambda b, i: (b, jnp.minimum(n_q * (i + 1), nb - 1), 0))
        band = _band_bias(blk)
        in_specs = [smem, qspec, _const_spec(band.shape), prev, cur, nxt, prev, cur, nxt, ctx_spec, ctx_spec]
        args = (sink, q, band, k, k, k, v, v, v, k_ctx, v_ctx)
    else:
        in_specs = [smem, qspec, ctx_spec, ctx_spec]
        args = (sink, q, k_ctx, v_ctx)
    return pl.pallas_call(
        functools.partial(_attn_kernel, local=local),
        grid=(B, steps),
        in_specs=in_specs,
        out_specs=qspec,
        out_shape=jax.ShapeDtypeStruct((B, N, BRANCH_WIDTH), BF16),
        compiler_params=_params(("parallel", "parallel")),
        name="attn_local" if local else "attn_ctx",
    )(*args)


def _gla_kernel(kqf_ref, vf_ref, gf_ref, kqb_ref, vb_ref, gb_ref, s0_ref,
                of_ref, ob_ref, sfin_ref, st_ref):
    i = pl.program_id(1)
    T = kqf_ref.shape[1]
    n_chunk = T // GLA_CHUNK

    n_seq = kqf_ref.shape[0]

    @pl.when(i == 0)
    def _():
        st_ref[...] = s0_ref[...]

    cum_rows = min(T, GLA_CUM_ROWS)
    ri = lax.broadcasted_iota(jnp.int32, (cum_rows, cum_rows), 0)
    ci = lax.broadcasted_iota(jnp.int32, (cum_rows, cum_rows), 1)
    same_chunk = (ri // GLA_CHUNK) == (ci // GLA_CHUNK)
    li = lax.broadcasted_iota(jnp.int32, (GLA_CHUNK, 2 * GLA_CHUNK), 0)
    mi = lax.broadcasted_iota(jnp.int32, (GLA_CHUNK, 2 * GLA_CHUNK), 1) % GLA_CHUNK
    first_head_lanes = lax.broadcasted_iota(jnp.int32, (GLA_CHUNK, LANES), 1) < GLA_DK
    own_block = ((lax.broadcasted_iota(jnp.int32, (2 * GLA_DV, LANES), 0) < GLA_DV)
                 == (lax.broadcasted_iota(jnp.int32, (2 * GLA_DV, LANES), 1) < GLA_DK))

    dirs = ((kqf_ref, vf_ref, gf_ref, of_ref), (kqb_ref, vb_ref, gb_ref, ob_ref))
    cums = {}
    for d, (kq_ref, v_ref, g_ref, o_ref) in enumerate(dirs):
        tri_full = jnp.where(same_chunk & ((ci >= ri) if d == 1 else (ci <= ri)), 1.0, 0.0).astype(BF16)
        for b in range(n_seq):
            cums[b, d] = jnp.concatenate([_split_dot_left(tri_full, g_ref[b, r0:r0 + cum_rows])
                                          for r0 in range(0, T, cum_rows)], axis=0)

    def chunk_of(d, step):
        return n_chunk - 1 - step if d == 1 else step

    def operands(b, d, p, c, anchor):
        kq_ref = dirs[d][0]
        lanes = slice(p * LANES, (p + 1) * LANES)
        k_lanes = slice(GLA_HEADS * GLA_DK + p * LANES, GLA_HEADS * GLA_DK + (p + 1) * LANES)
        rows = slice(c * GLA_CHUNK, (c + 1) * GLA_CHUNK)
        ref_row = GLA_CHUNK // 2 - 1 if d == 1 else GLA_CHUNK // 2
        tot_row = 0 if d == 1 else GLA_CHUNK - 1
        bc = cums[b, d][rows, lanes]
        if anchor is not None:
            bc = bc + anchor
        q2 = kq_ref[b, rows, lanes]
        k2 = kq_ref[b, rows, k_lanes]
        b_ref = bc[ref_row:ref_row + 1]
        b_tot = bc[tot_row:tot_row + 1]
        ke = k2 * jnp.exp(b_ref - bc)
        return dict(
            qe=(q2 * jnp.exp(bc - b_ref)).astype(BF16),
            q_in=(q2 * jnp.exp(bc)).astype(BF16),
            k_out=(k2 * jnp.exp(b_tot - bc)).astype(BF16),
            decay=jnp.exp(b_tot),
            ke=jnp.concatenate([jnp.where(first_head_lanes, ke, 0.0),
                                jnp.where(first_head_lanes, 0.0, ke)], axis=0).astype(BF16))

    ops = {}
    anchors = {}

    def score_group(gi, steps):
        anchor = anchors.get(gi - 2)
        for step in steps:
            for b in range(n_seq):
                for d in range(2):
                    for p in range(GLA_HEADS // 2):
                        u = operands(b, d, p, chunk_of(d, step), anchor)
                        keep = (mi >= li) if d == 1 else (mi <= li)
                        scores = _dot_nt(u["qe"], u["ke"])
                        u["pm"] = jnp.where(keep, scores, 0.0).astype(BF16)
                        ops[b, d, p, step] = u
        anchors[gi] = jnp.minimum(jnp.abs(scores[:1]), 0.0)

    zeros_v = jnp.zeros((GLA_CHUNK, GLA_DV), BF16)

    def scan_group(steps):
        for step, b, (d, (kq_ref, v_ref, g_ref, o_ref)), p in itertools.product(
                steps, range(n_seq), enumerate(dirs), range(GLA_HEADS // 2)):
            c = chunk_of(d, step)
            rows = slice(c * GLA_CHUNK, (c + 1) * GLA_CHUNK)
            lanes = slice(p * 2 * GLA_DV, (p + 1) * 2 * GLA_DV)
            u = ops.pop((b, d, p, step))
            v2 = v_ref[b, rows, lanes]
            v_diag = jnp.concatenate(
                [jnp.concatenate([v2[:, :GLA_DV], zeros_v], axis=1),
                 jnp.concatenate([zeros_v, v2[:, GLA_DV:]], axis=1)], axis=0)
            st = st_ref[b, d, p]
            o = _dot(u["pm"], v_diag) + _dot_nt(u["q_in"], st.astype(BF16))
            o_ref[b, rows, lanes] = o.astype(o_ref.dtype)
            st_ref[b, d, p] = st * u["decay"] + jnp.where(own_block, _dot_tn(v2, u["k_out"]), 0.0)

    groups = [range(s, min(s + GLA_GROUP_STEPS, n_chunk)) for s in range(0, n_chunk, GLA_GROUP_STEPS)]
    score_group(0, groups[0])
    for gi, steps in enumerate(groups):
        if gi + 1 < len(groups):
            score_group(gi + 1, groups[gi + 1])
        scan_group(steps)

    @pl.when(i == pl.num_programs(1) - 1)
    def _():
        sfin_ref[...] = st_ref[...]


def _split_dot_left(m, a):
    hi = a.astype(BF16)
    lo = (a - hi.astype(F32)).astype(BF16)
    return _dot(m, hi) + _dot(m, lo)


def _gla(kq, v, g2, s0, T):
    B, N, _ = kq.shape
    nt = N // T
    nb = GLA_SEQS_PER_STEP
    fwd = lambda w: pl.BlockSpec((nb, T, w), lambda b, i: (b, i, 0))
    bwd = lambda w: pl.BlockSpec((nb, T, w), lambda b, i: (b, nt - 1 - i, 0))
    g_fwd = pl.BlockSpec((nb, T, 256), lambda b, i: (b, i, 0))
    g_bwd = pl.BlockSpec((nb, T, 256), lambda b, i: (b, nt - 1 - i, 1))
    st_shape = GLA_STATE_SHAPE
    st_spec = pl.BlockSpec((nb,) + st_shape, lambda b, i: (b, 0, 0, 0, 0))
    return pl.pallas_call(
        _gla_kernel,
        grid=(B // nb, nt),
        in_specs=[fwd(512), fwd(512), g_fwd, bwd(512), bwd(512), g_bwd, st_spec],
        out_specs=[fwd(512), bwd(512), st_spec],
        out_shape=[jax.ShapeDtypeStruct((B, N, 512), BF16), jax.ShapeDtypeStruct((B, N, 512), BF16),
                   jax.ShapeDtypeStruct((B,) + st_shape, F32)],
        scratch_shapes=[pltpu.VMEM((nb,) + st_shape, F32)],
        compiler_params=_params(("parallel", "arbitrary")),
        name="gla",
    )(kq, v, g2, kq, v, g2, s0)


def _merge_kernel(x_ref, sh_ref, sc_ref, g1_ref, a_ref, b_ref, of_ref, ob_ref, r_ref,
                  gn_ref, wg_ref, wbr_ref, wo_ref, o_ref):
    def gated_mix(rows):
        hb = _modulated(x_ref[0, rows], sh_ref[0], sc_ref[0]).astype(BF16)
        o = of_ref[0, rows].astype(F32) + ob_ref[0, rows].astype(F32)
        cs = []
        for h in range(GLA_HEADS):
            oh = o[:, h * LANES:(h + 1) * LANES]
            cs.append(oh * lax.rsqrt(jnp.mean(oh * oh, axis=-1, keepdims=True) + EPS) * gn_ref[...])
        r = r_ref[0, rows].astype(F32)
        c_out = (jnp.concatenate(cs, axis=1) * (r * jax.nn.sigmoid(r))).astype(BF16)
        branches = (a_ref[0, rows], b_ref[0, rows], c_out)
        mixed = None
        for k in range(N_BRANCH):
            gate = jax.nn.sigmoid(_dot(hb, wg_ref[:, k * D_MODEL:(k + 1) * D_MODEL])).astype(BF16)
            term = gate.astype(F32) * _dot(branches[k], wbr_ref[k])
            mixed = term if mixed is None else mixed + term
        return mixed.astype(BF16)

    def project_out(rows, mixed):
        o_ref[0, rows] = x_ref[0, rows] + g1_ref[0] * _dot(mixed, wo_ref[...])

    prev = None
    for rows in _sub_tiles(x_ref.shape[1]):
        mixed = gated_mix(rows)
        if prev is not None:
            project_out(*prev)
        prev = (rows, mixed)
    project_out(*prev)


def _merge(x, mods, mod_row, a, b, o_f, o_b, r, lw, tm):
    B, N, _ = x.shape
    tok = lambda w: pl.BlockSpec((1, tm, w), lambda b_, i: (b_, i, 0))
    consts = [lw["gla_norm"], lw["w_g"], lw["w_br"], lw["w_o"]]
    return pl.pallas_call(
        _merge_kernel,
        grid=(B, N // tm),
        in_specs=[tok(D_MODEL)] + [_mod_spec(j, mod_row) for j in range(3)] + [tok(512)] * 5
                 + [_const_spec(c.shape) for c in consts],
        out_specs=tok(D_MODEL),
        out_shape=jax.ShapeDtypeStruct((B, N, D_MODEL), F32),
        compiler_params=_params(("parallel", "parallel")),
        name="merge",
    )(x, mods, mods, mods, a, b, o_f, o_b, r, *consts)


def _ffn_kernel(x_ref, sh_ref, sc_ref, g2_ref, w1_ref, w2_ref, o_ref):
    x = x_ref[0]
    hb = _modulated(x, sh_ref[0], sc_ref[0]).astype(BF16)
    t = jnp.maximum(_dot(hb, w1_ref[...]), 0.0)
    f = _dot((t * t).astype(BF16), w2_ref[...])
    o_ref[0] = x + g2_ref[0] * f


def _ffn(x, mods, mod_row, lw, tm):
    B, N, _ = x.shape
    tok = pl.BlockSpec((1, tm, D_MODEL), lambda b, i: (b, i, 0))
    return pl.pallas_call(
        _ffn_kernel,
        grid=(B, N // tm),
        in_specs=[tok] + [_mod_spec(j, mod_row) for j in (3, 4, 5)]
                 + [_const_spec(lw["w_ff1"].shape), _const_spec(lw["w_ff2"].shape)],
        out_specs=tok,
        out_shape=jax.ShapeDtypeStruct((B, N, D_MODEL), F32),
        compiler_params=_params(("parallel", "parallel")),
        name="ffn",
    )(x, mods, mods, mods, lw["w_ff1"], lw["w_ff2"])


def _rope_tables(n_tokens):
    lane = np.arange(LANES)
    d = lane % HEAD_DIM
    use_col = (d // (HEAD_DIM // 2)) == 1
    second = ((d % (HEAD_DIM // 2)) // (HEAD_DIM // 4)) == 1
    f = d % (HEAD_DIM // 4)
    n_freq = HEAD_DIM // 4
    freqs = ROPE_BASE ** (-jnp.arange(n_freq, dtype=F32) / n_freq)
    t = jnp.arange(n_tokens)
    row = (t // GRID_W).astype(F32)
    col = (t % GRID_W).astype(F32)
    ang_row = row[:, None] * freqs[None, :]
    ang_col = col[:, None] * freqs[None, :]
    ang = jnp.where(jnp.asarray(use_col)[None, :], ang_col[:, f], ang_row[:, f])
    cos, sin = jnp.cos(ang), jnp.sin(ang)
    second = jnp.asarray(second)[None, :]
    return cos, jnp.where(second, sin, 0.0), jnp.where(second, 0.0, -sin)


def _identity_tables(n_tokens):
    return (jnp.ones((n_tokens, LANES), F32), jnp.zeros((n_tokens, LANES), F32),
            jnp.zeros((n_tokens, LANES), F32))


def _layer_weights(l, w_in, q_norm, k_norm, sgu_norm, w_sgu, b_sgu, w_gate_f, b_gate_f,
                   w_gate_b, b_gate_b, gla_norm, w_br, w_o, w_ff1, w_ff2):
    w = w_in[l]
    col = lambda i: w[:, _OFFS[i]:_OFFS[i + 1]]
    wq = col(2).reshape(D_MODEL, SWA_KV_HEADS, SWA_GROUP, HEAD_DIM).transpose(0, 2, 1, 3).reshape(D_MODEL, 512)
    w_z = jnp.zeros((D_MODEL, LANES), F32).at[:, :2 * GLA_RANK].set(w[:, _OFFS[9]:_OFFS[11]])
    n_dk = GLA_HEADS * GLA_DK
    w2 = jnp.zeros((LANES, 2 * n_dk), F32)
    w2 = w2.at[:GLA_RANK, :n_dk].set(w_gate_f[l]).at[GLA_RANK:2 * GLA_RANK, n_dk:].set(w_gate_b[l])
    b2 = jnp.concatenate([b_gate_f[l], b_gate_b[l]])[None, :]
    wbr1 = w_br[l, 1].reshape(SWA_KV_HEADS, SWA_GROUP, HEAD_DIM, D_MODEL).transpose(1, 0, 2, 3).reshape(512, D_MODEL)
    return {
        "w_uv": w[:, _OFFS[0]:_OFFS[2]].astype(BF16),
        "w_b": jnp.concatenate([wq, col(3), col(4)], axis=1).astype(BF16),
        "w_c": w[:, _OFFS[5]:_OFFS[9]].astype(BF16),
        "w_z": w_z.astype(BF16),
        "w_g": col(11).astype(BF16),
        "w2": w2.astype(BF16),
        "b2": b2,
        "sgu_norm": sgu_norm[l][None, :],
        "w_sgu": w_sgu[l].astype(BF16),
        "b_sgu": jnp.broadcast_to(b_sgu[l][:, :, None], (SGU_GROUPS, SGU_CHUNK, LANES)),
        "q_norm": jnp.tile(q_norm[l], 2)[None, :],
        "k_norm": jnp.tile(k_norm[l], 2)[None, :],
        "gla_norm": gla_norm[l][None, :],
        "w_br": jnp.stack([w_br[l, 0], wbr1, w_br[l, 2]]).astype(BF16),
        "w_o": w_o[l].astype(BF16),
        "w_ff1": w_ff1[l].astype(BF16),
        "w_ff2": w_ff2[l].astype(BF16),
    }


def kernel(x, c, ctx, c_ctx, w_ada, b_ada, w_in, q_norm, k_norm, sink, sgu_norm, w_sgu, b_sgu,
           w_gate_f, b_gate_f, w_gate_b, b_gate_b, gla_norm, w_br, w_o, w_ff1, w_ff2):
    B, N, _ = x.shape
    C = ctx.shape[1]
    depth = w_ada.shape[0]
    n_cond = 16
    tm_lat = 1024
    tm_ctx = min(1024, B * C)
    gla_tile = 1024

    cond = jnp.zeros((n_cond, D_MODEL), F32).at[:B].set(c).at[B].set(c_ctx)
    mods = _adaln(cond, w_ada, b_ada)
    rope_lat = _rope_tables(N)
    rope_ctx = _identity_tables(B * C)
    zero_state = jnp.zeros((B,) + GLA_STATE_SHAPE, F32)

    flat = lambda t: t.reshape(1, B * C, t.shape[-1])
    per_batch = lambda t: t.reshape(B, C, t.shape[-1])
    ctx = flat(ctx)

    for l in range(depth):
        lw = _layer_weights(l, w_in, q_norm, k_norm, sgu_norm, w_sgu, b_sgu, w_gate_f, b_gate_f,
                            w_gate_b, b_gate_b, gla_norm, w_br, w_o, w_ff1, w_ff2)
        m = mods[l].reshape(n_cond * 6, 1, D_MODEL)

        (ca, cq, ck, cv, ckq, cvc, cr, cg2) = _proj(ctx, m, B, rope_ctx, lw, tm_ctx)
        ck, cv = per_batch(ck), per_batch(cv)
        oc_f, oc_b, s_ctx = _gla(per_batch(ckq), per_batch(cvc), per_batch(cg2), zero_state, min(gla_tile, C))

        (a, q, k, v, kq, vc, r, g2) = _proj(x, m, None, rope_lat, lw, tm_lat)
        b_out = _attn(q, k, v, ck, cv, sink[l], local=True)
        o_f, o_b, _ = _gla(kq, vc, g2, s_ctx, gla_tile)
        x_new = _merge(x, m, None, a, b_out, o_f, o_b, r, lw, tm_lat)
        x_new = _ffn(x_new, m, None, lw, tm_lat)

        if l < depth - 1:
            cb_out = _attn(per_batch(cq), None, None, ck, cv, sink[l], local=False)
            ctx_new = _merge(ctx, m, B, ca, flat(cb_out), flat(oc_f), flat(oc_b), cr, lw, tm_ctx)
            ctx = _ffn(ctx_new, m, B, lw, tm_ctx)
        x = x_new

    return x
```

```python
import functools
import itertools

import jax
import jax.numpy as jnp
import numpy as np
from jax import lax
from jax.experimental import pallas as pl
from jax.experimental.pallas import tpu as pltpu

D_MODEL = 1024
GRID_W = 64
HEAD_DIM = 64
BRANCH_WIDTH = 512
N_BRANCH = 3
SGU_GROUPS = 4
SGU_CHUNK = 128
SWA_Q_HEADS = 8
SWA_KV_HEADS = 2
SWA_GROUP = 4
SWA_WINDOW = 128
SWA_BLOCK = 128
ROPE_BASE = 10000.0
GLA_HEADS = 4
GLA_DK = 64
GLA_DV = 128
GLA_RANK = 16
GLA_NORMALIZER = 16.0
GLA_CHUNK = 64
FFN_DIM = 4 * D_MODEL
EPS = 1e-6

LANES = 128
VMEM_LIMIT = 56 * 1024 * 1024
NEG_BIG = -1e30
LOG2E = 1.4426950408889634
TOKEN_BLOCK = 1024
PROJ_SUBTILE = 256
MERGE_SUBTILE = 256
ATTN_Q_BLOCKS = 16
GLA_TILE = 2048
GLA_CUM_ROWS = 256
GLA_SEQS_PER_STEP = 1
GLA_GROUP_STEPS = 1
GLA_STATE_SHAPE = (2, GLA_HEADS // 2, 2 * GLA_DV, LANES)
BF16 = jnp.bfloat16
F32 = jnp.float32

_SPLITS = (512, 512, 512, 128, 128, 256, 256, 512, 512, 16, 16, 3072)
_OFFS = np.concatenate([[0], np.cumsum(_SPLITS)]).tolist()


def _dot(a, b):
    return jnp.dot(a, b, preferred_element_type=F32)


def _dot_nt(a, b):
    return lax.dot_general(a, b, (((1,), (1,)), ((), ())), preferred_element_type=F32)


def _dot_tn(a, b):
    return lax.dot_general(a, b, (((0,), (0,)), ((), ())), preferred_element_type=F32)


def _const_spec(shape):
    nd = len(shape)
    return pl.BlockSpec(shape, lambda *_: (0,) * nd, pipeline_mode=pl.Buffered(1))


def _params(sem):
    return pltpu.CompilerParams(dimension_semantics=sem, vmem_limit_bytes=VMEM_LIMIT)


def _adaln_kernel(cond_ref, w_ref, b_ref, o_ref):
    cond = cond_ref[...]
    s = (cond * jax.nn.sigmoid(cond)).astype(BF16)
    o_ref[0] = _dot(s, w_ref[0].astype(BF16)) + b_ref[0]


def _adaln(cond, w_ada, b_ada):
    depth = w_ada.shape[0]
    rows = cond.shape[0]
    tn = 3072
    return pl.pallas_call(
        _adaln_kernel,
        grid=(depth, 6 * D_MODEL // tn),
        in_specs=[
            pl.BlockSpec((rows, D_MODEL), lambda l, j: (0, 0)),
            pl.BlockSpec((1, D_MODEL, tn), lambda l, j: (l, 0, j)),
            pl.BlockSpec((1, 1, tn), lambda l, j: (l, 0, j)),
        ],
        out_specs=pl.BlockSpec((1, rows, tn), lambda l, j: (l, 0, j)),
        out_shape=jax.ShapeDtypeStruct((depth, rows, 6 * D_MODEL), F32),
        compiler_params=_params(("arbitrary", "arbitrary")),
        name="adaln",
    )(cond, w_ada, b_ada.reshape(depth, 1, 6 * D_MODEL))


def _sub_tiles(n_rows, size=PROJ_SUBTILE):
    return [slice(r0, r0 + size) for r0 in range(0, n_rows, size)]


def _modulated(x, shift, scale):
    ms = jnp.mean(x * x, axis=-1, keepdims=True)
    return x * lax.rsqrt(ms + EPS) * (1.0 + scale) + shift


def _proj_kernel(x_ref, sh_ref, sc_ref, cos_ref, sa_ref, sb_ref,
                 wuv_ref, wb_ref, wc_ref, wz_ref, w2_ref, b2_ref,
                 sgun_ref, ws_ref, bs_ref, qn_ref, kn_ref,
                 a_ref, q_ref, k_ref, v_ref, kq_ref, vc_ref, r_ref, g2_ref):
    ri = lax.broadcasted_iota(jnp.int32, (2 * LANES, 2 * LANES), 0) // HEAD_DIM
    ci = lax.broadcasted_iota(jnp.int32, (2 * LANES, 2 * LANES), 1) // HEAD_DIM
    group_ones = jnp.where(ri == ci, 1.0, 0.0).astype(BF16)

    tiles = _sub_tiles(x_ref.shape[1])

    hbs = {rows.start: _modulated(x_ref[0, rows], sh_ref[0], sc_ref[0]).astype(BF16) for rows in tiles}
    zs = {0: _dot(hbs[0], wz_ref[...]).astype(BF16)}

    def big(rows):
        hb = hbs[rows.start]
        t = dict(uv=_dot(hb, wuv_ref[...]))
        nxt = rows.stop
        if nxt in hbs:
            zs[nxt] = _dot(hbs[nxt], wz_ref[...]).astype(BF16)
        t["bq"] = _dot(hb, wb_ref[...])
        t["z"] = zs[rows.start]
        return t

    def mixer_c_projection(rows):
        cp = _dot(hbs[rows.start], wc_ref[...])
        n_q = GLA_HEADS * GLA_DK
        kq_ref[0, rows, :n_q] = cp[:, :n_q] * GLA_DK ** -0.5
        kq_ref[0, rows, n_q:] = cp[:, n_q:512]
        vc_ref[0, rows] = cp[:, 512:1024].astype(vc_ref.dtype)
        r_ref[0, rows] = cp[:, 1024:1536].astype(r_ref.dtype)

    def prepare(rows, t):
        bq, uv = t["bq"], t["uv"]
        v_ref[0, rows] = bq[:, 640:768].astype(v_ref.dtype)
        t["sq"] = [(xs * xs).astype(BF16) for xs in (bq[:, :256], bq[:, 256:512], bq[:, 512:768])]
        t["u"] = jax.nn.gelu(uv[:, :BRANCH_WIDTH])
        va = jax.nn.gelu(uv[:, BRANCH_WIDTH:])
        t["vg"] = []
        for g in range(SGU_GROUPS):
            lanes = slice(g * LANES, (g + 1) * LANES)
            vg = va[:, lanes]
            vg = vg * lax.rsqrt(jnp.mean(vg * vg, axis=-1, keepdims=True) + EPS) * sgun_ref[:, lanes]
            t["vg"].append(vg.astype(BF16))

    def small(rows, t):
        bq = t["bq"]
        cos, sa, sb = cos_ref[rows], sa_ref[rows], sb_ref[rows]

        def rope(y):
            return y * cos + pltpu.roll(y, 16, 1) * sa + pltpu.roll(y, LANES - 16, 1) * sb

        logit = _dot(t["z"], w2_ref[...]) + b2_ref[...]
        log_sig = jnp.minimum(logit, 0.0) - jnp.log(1.0 + jnp.exp(-jnp.abs(logit)))
        g2_ref[0, rows] = log_sig * (1.0 / GLA_NORMALIZER)
        inv = [lax.rsqrt(_dot(sq, group_ones) * (1.0 / HEAD_DIM) + EPS) for sq in t["sq"]]
        for s in range(BRANCH_WIDTH // LANES):
            lanes = slice(s * LANES, (s + 1) * LANES)
            qn = bq[:, lanes] * inv[s // 2][:, (s % 2) * LANES:(s % 2 + 1) * LANES] * qn_ref[...]
            q_ref[0, rows, lanes] = (rope(qn) * (HEAD_DIM ** -0.5 * LOG2E)).astype(q_ref.dtype)
        k_ref[0, rows] = rope(bq[:, 512:640] * inv[2][:, :LANES] * kn_ref[...]).astype(k_ref.dtype)
        n_chunk = (rows.stop - rows.start) // SGU_CHUNK
        for g in range(SGU_GROUPS):
            lanes = slice(g * LANES, (g + 1) * LANES)
            vg = jnp.concatenate([t["vg"][g][c * SGU_CHUNK:(c + 1) * SGU_CHUNK] for c in range(n_chunk)], axis=1)
            s_all = _dot(ws_ref[g], vg)
            for c in range(n_chunk):
                cr = slice(c * SGU_CHUNK, (c + 1) * SGU_CHUNK)
                out_rows = slice(rows.start + c * SGU_CHUNK, rows.start + (c + 1) * SGU_CHUNK)
                s = s_all[:, c * LANES:(c + 1) * LANES] + bs_ref[g]
                a_ref[0, out_rows, lanes] = (t["u"][cr, lanes] * s).astype(a_ref.dtype)

    prev = None
    for rows in tiles:
        cur = big(rows)
        if rows is not tiles[-1]:
            mixer_c_projection(rows)
        if prev is not None:
            small(*prev)
        prepare(rows, cur)
        prev = (rows, cur)
    small(*prev)
    mixer_c_projection(tiles[-1])


def _mod_spec(j, mod_row):
    if mod_row is None:
        return pl.BlockSpec((1, 1, D_MODEL), lambda b, i: (b * 6 + j, 0, 0))
    return pl.BlockSpec((1, 1, D_MODEL), lambda b, i: (mod_row * 6 + j, 0, 0))


def _proj(x, mods, mod_row, rope, lw, tm):
    B, N, _ = x.shape
    cos, sa, sb = rope
    tok = lambda w: pl.BlockSpec((1, tm, w), lambda b, i: (b, i, 0))
    tab = pl.BlockSpec((tm, LANES), lambda b, i: (i, 0))
    consts = [lw["w_uv"], lw["w_b"], lw["w_c"], lw["w_z"], lw["w2"], lw["b2"],
              lw["sgu_norm"], lw["w_sgu"], lw["b_sgu"], lw["q_norm"], lw["k_norm"]]
    out_w = [(512, BF16), (512, BF16), (128, BF16), (128, BF16), (512, F32), (512, BF16),
             (512, BF16), (512, F32)]
    return pl.pallas_call(
        _proj_kernel,
        grid=(B, N // tm),
        in_specs=[tok(D_MODEL), _mod_spec(0, mod_row), _mod_spec(1, mod_row), tab, tab, tab]
                 + [_const_spec(c.shape) for c in consts],
        out_specs=[tok(w) for w, _ in out_w],
        out_shape=[jax.ShapeDtypeStruct((B, N, w), dt) for w, dt in out_w],
        compiler_params=_params(("parallel", "parallel")),
        name="proj",
    )(x, mods, mods, cos, sa, sb, *consts)


def _attn_kernel(sink_ref, q_ref, *refs, local):
    blk = SWA_BLOCK
    n_q = q_ref.shape[1] // blk
    n_slab = BRANCH_WIDTH // LANES
    if local:
        band_ref, kp_ref, kc_ref, kn_ref, vp_ref, vc_ref, vn_ref, kx_ref, vx_ref, o_ref = refs
        k_loc = [kp_ref[0]] + [kc_ref[0, j * blk:(j + 1) * blk] for j in range(n_q)] + [kn_ref[0]]
        v_loc = [vp_ref[0]] + [vc_ref[0, j * blk:(j + 1) * blk] for j in range(n_q)] + [vn_ref[0]]
        keys = [jnp.concatenate(k_loc[j:j + 3] + [kx_ref[0]], axis=0) for j in range(n_q)]
        vals = [jnp.concatenate(v_loc[j:j + 3] + [vx_ref[0]], axis=0) for j in range(n_q)]
        i = pl.program_id(1)
        no_prev = jnp.where(i == 0, NEG_BIG, 0.0)
        no_next = jnp.where(i == pl.num_programs(1) - 1, NEG_BIG, 0.0)
        bias_prev = [band_ref[:, :blk] + no_prev if j == 0 else band_ref[:, :blk] for j in range(n_q)]
        bias_next = [band_ref[:, blk:] + no_next if j == n_q - 1 else band_ref[:, blk:] for j in range(n_q)]
    else:
        kx_ref, vx_ref, o_ref = refs
        keys, vals = [kx_ref[0]] * n_q, [vx_ref[0]] * n_q
    head0_lanes = lax.broadcasted_iota(jnp.int32, (blk, LANES), 1) < HEAD_DIM
    val_lane = lax.broadcasted_iota(jnp.int32, vals[0].shape, 1)

    scores = {}

    def score_block(j):
        rows = slice(j * blk, (j + 1) * blk)
        for h in range(SWA_KV_HEADS):
            own = head0_lanes if h == 0 else ~head0_lanes
            qh = [jnp.where(own, q_ref[0, rows, s * LANES:(s + 1) * LANES], 0.0).astype(BF16)
                  for s in range(n_slab)]
            scores[j, h] = _dot_nt(jnp.concatenate(qh, axis=0), keys[j])

    score_block(0)
    for j in range(n_q):
        if j + 1 < n_q:
            score_block(j + 1)
        outs = {}
        for h in range(SWA_KV_HEADS):
            ps, sink_terms = [], []
            for s in range(n_slab):
                sc = scores[j, h][s * blk:(s + 1) * blk]
                if local:
                    sc = jnp.concatenate([sc[:, :blk] + bias_prev[j], sc[:, blk:2 * blk],
                                          sc[:, 2 * blk:3 * blk] + bias_next[j], sc[:, 3 * blk:]], axis=1)
                sink = sink_ref[h * SWA_GROUP + s] * LOG2E
                m = jnp.maximum(jnp.max(sc, axis=-1, keepdims=True), sink)
                ps.append(jnp.exp2(sc - m).astype(BF16))
                sink_terms.append(jnp.exp2(sink - m))
            own = (val_lane < HEAD_DIM) if h == 0 else (val_lane >= HEAD_DIM)
            pv = _dot(jnp.concatenate(ps, axis=0), jnp.where(own, vals[j], jnp.ones_like(vals[j])))
            for s in range(n_slab):
                pv_s = pv[s * blk:(s + 1) * blk]
                denom = pltpu.roll(pv_s, HEAD_DIM, 1) + sink_terms[s]
                outs[s, h] = pv_s / denom
        for s in range(n_slab):
            o_ref[0, j * blk:(j + 1) * blk, s * LANES:(s + 1) * LANES] = jnp.where(
                head0_lanes, outs[s, 0], outs[s, 1]).astype(o_ref.dtype)


def _band_bias(blk):
    qi = np.arange(blk)[:, None]
    kj = np.arange(blk)[None, :]
    prev_ok = np.abs(kj - blk - qi) <= SWA_WINDOW
    next_ok = np.abs(kj + blk - qi) <= SWA_WINDOW
    return jnp.asarray(np.where(np.concatenate([prev_ok, next_ok], axis=1), 0.0, NEG_BIG), F32)


def _attn(q, k, v, k_ctx, v_ctx, sink, local):
    B, N, _ = q.shape
    C = k_ctx.shape[1]
    blk = SWA_BLOCK
    nb = N // blk
    n_q = min(ATTN_Q_BLOCKS, nb)
    steps = nb // n_q
    qspec = pl.BlockSpec((1, n_q * blk, BRANCH_WIDTH), lambda b, i: (b, i, 0))
    ctx_spec = pl.BlockSpec((1, C, LANES), lambda b, i: (b, 0, 0))
    smem = pl.BlockSpec(memory_space=pltpu.SMEM)
    if local:
        prev = pl.BlockSpec((1, blk, LANES), lambda b, i: (b, jnp.maximum(n_q * i - 1, 0), 0))
        cur = pl.BlockSpec((1, n_q * blk, LANES), lambda b, i: (b, i, 0))
        nxt = pl.BlockSpec((1, blk, LANES), lambda b, i: (b, jnp.minimum(n_q * (i + 1), nb - 1), 0))
        band = _band_bias(blk)
        in_specs = [smem, qspec, _const_spec(band.shape), prev, cur, nxt, prev, cur, nxt, ctx_spec, ctx_spec]
        args = (sink, q, band, k, k, k, v, v, v, k_ctx, v_ctx)
    else:
        in_specs = [smem, qspec, ctx_spec, ctx_spec]
        args = (sink, q, k_ctx, v_ctx)
    return pl.pallas_call(
        functools.partial(_attn_kernel, local=local),
        grid=(B, steps),
        in_specs=in_specs,
        out_specs=qspec,
        out_shape=jax.ShapeDtypeStruct((B, N, BRANCH_WIDTH), BF16),
        compiler_params=_params(("parallel", "parallel")),
        name="attn_local" if local else "attn_ctx",
    )(*args)


def _gla_kernel(kqf_ref, vf_ref, gf_ref, kqb_ref, vb_ref, gb_ref, s0_ref,
                of_ref, ob_ref, sfin_ref, st_ref):
    i = pl.program_id(1)
    T = kqf_ref.shape[1]
    n_chunk = T // GLA_CHUNK

    n_seq = kqf_ref.shape[0]

    @pl.when(i == 0)
    def _():
        st_ref[...] = s0_ref[...]

    cum_rows = min(T, GLA_CUM_ROWS)
    ri = lax.broadcasted_iota(jnp.int32, (cum_rows, cum_rows), 0)
    ci = lax.broadcasted_iota(jnp.int32, (cum_rows, cum_rows), 1)
    same_chunk = (ri // GLA_CHUNK) == (ci // GLA_CHUNK)
    li = lax.broadcasted_iota(jnp.int32, (GLA_CHUNK, 2 * GLA_CHUNK), 0)
    mi = lax.broadcasted_iota(jnp.int32, (GLA_CHUNK, 2 * GLA_CHUNK), 1) % GLA_CHUNK
    first_head_lanes = lax.broadcasted_iota(jnp.int32, (GLA_CHUNK, LANES), 1) < GLA_DK
    own_block = ((lax.broadcasted_iota(jnp.int32, (2 * GLA_DV, LANES), 0) < GLA_DV)
                 == (lax.broadcasted_iota(jnp.int32, (2 * GLA_DV, LANES), 1) < GLA_DK))

    dirs = ((kqf_ref, vf_ref, gf_ref, of_ref), (kqb_ref, vb_ref, gb_ref, ob_ref))
    cums = {}
    for d, (kq_ref, v_ref, g_ref, o_ref) in enumerate(dirs):
        tri_full = jnp.where(same_chunk & ((ci >= ri) if d == 1 else (ci <= ri)), 1.0, 0.0).astype(BF16)
        for b in range(n_seq):
            cums[b, d] = jnp.concatenate([_split_dot_left(tri_full, g_ref[b, r0:r0 + cum_rows])
                                          for r0 in range(0, T, cum_rows)], axis=0)

    def chunk_of(d, step):
        return n_chunk - 1 - step if d == 1 else step

    def operands(b, d, p, c, anchor):
        kq_ref = dirs[d][0]
        lanes = slice(p * LANES, (p + 1) * LANES)
        k_lanes = slice(GLA_HEADS * GLA_DK + p * LANES, GLA_HEADS * GLA_DK + (p + 1) * LANES)
        rows = slice(c * GLA_CHUNK, (c + 1) * GLA_CHUNK)
        ref_row = GLA_CHUNK // 2 - 1 if d == 1 else GLA_CHUNK // 2
        tot_row = 0 if d == 1 else GLA_CHUNK - 1
        bc = cums[b, d][rows, lanes]
        if anchor is not None:
            bc = bc + anchor
        q2 = kq_ref[b, rows, lanes]
        k2 = kq_ref[b, rows, k_lanes]
        b_ref = bc[ref_row:ref_row + 1]
        b_tot = bc[tot_row:tot_row + 1]
        ke = k2 * jnp.exp(b_ref - bc)
        return dict(
            qe=(q2 * jnp.exp(bc - b_ref)).astype(BF16),
            q_in=(q2 * jnp.exp(bc)).astype(BF16),
            k_out=(k2 * jnp.exp(b_tot - bc)).astype(BF16),
            decay=jnp.exp(b_tot),
            ke=jnp.concatenate([jnp.where(first_head_lanes, ke, 0.0),
                                jnp.where(first_head_lanes, 0.0, ke)], axis=0).astype(BF16))

    ops = {}
    anchors = {}

    def score_group(gi, steps):
        anchor = anchors.get(gi - 2)
        for step in steps:
            for b in range(n_seq):
                for d in range(2):
                    for p in range(GLA_HEADS // 2):
                        u = operands(b, d, p, chunk_of(d, step), anchor)
                        keep = (mi >= li) if d == 1 else (mi <= li)
                        scores = _dot_nt(u["qe"], u["ke"])
                        u["pm"] = jnp.where(keep, scores, 0.0).astype(BF16)
                        ops[b, d, p, step] = u
        anchors[gi] = jnp.minimum(jnp.abs(scores[:1]), 0.0)

    zeros_v = jnp.zeros((GLA_CHUNK, GLA_DV), BF16)

    def scan_group(steps):
        for step, b, (d, (kq_ref, v_ref, g_ref, o_ref)), p in itertools.product(
                steps, range(n_seq), enumerate(dirs), range(GLA_HEADS // 2)):
            c = chunk_of(d, step)
            rows = slice(c * GLA_CHUNK, (c + 1) * GLA_CHUNK)
            lanes = slice(p * 2 * GLA_DV, (p + 1) * 2 * GLA_DV)
            u = ops.pop((b, d, p, step))
            v2 = v_ref[b, rows, lanes]
            v_diag = jnp.concatenate(
                [jnp.concatenate([v2[:, :GLA_DV], zeros_v], axis=1),
                 jnp.concatenate([zeros_v, v2[:, GLA_DV:]], axis=1)], axis=0)
            st = st_ref[b, d, p]
            o = _dot(u["pm"], v_diag) + _dot_nt(u["q_in"], st.astype(BF16))
            o_ref[b, rows, lanes] = o.astype(o_ref.dtype)
            st_ref[b, d, p] = st * u["decay"] + jnp.where(own_block, _dot_tn(v2, u["k_out"]), 0.0)

    groups = [range(s, min(s + GLA_GROUP_STEPS, n_chunk)) for s in range(0, n_chunk, GLA_GROUP_STEPS)]
    score_group(0, groups[0])
    for gi, steps in enumerate(groups):
        if gi + 1 < len(groups):
            score_group(gi + 1, groups[gi + 1])
        scan_group(steps)

    @pl.when(i == pl.num_programs(1) - 1)
    def _():
        sfin_ref[...] = st_ref[...]


def _split_dot_left(m, a):
    hi = a.astype(BF16)
    lo = (a - hi.astype(F32)).astype(BF16)
    return _dot(m, hi) + _dot(m, lo)


def _gla(kq, v, g2, s0, T):
    B, N, _ = kq.shape
    nt = N // T
    nb = GLA_SEQS_PER_STEP
    fwd = lambda w: pl.BlockSpec((nb, T, w), lambda b, i: (b, i, 0))
    bwd = lambda w: pl.BlockSpec((nb, T, w), lambda b, i: (b, nt - 1 - i, 0))
    g_fwd = pl.BlockSpec((nb, T, 256), lambda b, i: (b, i, 0))
    g_bwd = pl.BlockSpec((nb, T, 256), lambda b, i: (b, nt - 1 - i, 1))
    st_shape = GLA_STATE_SHAPE
    st_spec = pl.BlockSpec((nb,) + st_shape, lambda b, i: (b, 0, 0, 0, 0))
    return pl.pallas_call(
        _gla_kernel,
        grid=(B // nb, nt),
        in_specs=[fwd(512), fwd(512), g_fwd, bwd(512), bwd(512), g_bwd, st_spec],
        out_specs=[fwd(512), bwd(512), st_spec],
        out_shape=[jax.ShapeDtypeStruct((B, N, 512), BF16), jax.ShapeDtypeStruct((B, N, 512), BF16),
                   jax.ShapeDtypeStruct((B,) + st_shape, F32)],
        scratch_shapes=[pltpu.VMEM((nb,) + st_shape, F32)],
        compiler_params=_params(("parallel", "arbitrary")),
        name="gla",
    )(kq, v, g2, kq, v, g2, s0)


def _merge_kernel(x_ref, sh_ref, sc_ref, g1_ref, a_ref, b_ref, of_ref, ob_ref, r_ref,
                  gn_ref, wg_ref, wbr_ref, wo_ref, o_ref):
    def gated_mix(rows):
        hb = _modulated(x_ref[0, rows], sh_ref[0], sc_ref[0]).astype(BF16)
        o = of_ref[0, rows].astype(F32) + ob_ref[0, rows].astype(F32)
        cs = []
        for h in range(GLA_HEADS):
            oh = o[:, h * LANES:(h + 1) * LANES]
            cs.append(oh * lax.rsqrt(jnp.mean(oh * oh, axis=-1, keepdims=True) + EPS) * gn_ref[...])
        r = r_ref[0, rows].astype(F32)
        c_out = (jnp.concatenate(cs, axis=1) * (r * jax.nn.sigmoid(r))).astype(BF16)
        branches = (a_ref[0, rows], b_ref[0, rows], c_out)
        mixed = None
        for k in range(N_BRANCH):
            gate = jax.nn.sigmoid(_dot(hb, wg_ref[:, k * D_MODEL:(k + 1) * D_MODEL]))
            term = gate * _dot(branches[k], wbr_ref[k])
            mixed = term if mixed is None else mixed + term
        return mixed.astype(BF16)

    def project_out(rows, mixed):
        o_ref[0, rows] = x_ref[0, rows] + g1_ref[0] * _dot(mixed, wo_ref[...])

    prev = None
    for rows in _sub_tiles(x_ref.shape[1], MERGE_SUBTILE):
        mixed = gated_mix(rows)
        if prev is not None:
            project_out(*prev)
        prev = (rows, mixed)
    project_out(*prev)


def _merge(x, mods, mod_row, a, b, o_f, o_b, r, lw, tm):
    B, N, _ = x.shape
    tok = lambda w: pl.BlockSpec((1, tm, w), lambda b_, i: (b_, i, 0))
    consts = [lw["gla_norm"], lw["w_g"], lw["w_br"], lw["w_o"]]
    return pl.pallas_call(
        _merge_kernel,
        grid=(B, N // tm),
        in_specs=[tok(D_MODEL)] + [_mod_spec(j, mod_row) for j in range(3)] + [tok(512)] * 5
                 + [_const_spec(c.shape) for c in consts],
        out_specs=tok(D_MODEL),
        out_shape=jax.ShapeDtypeStruct((B, N, D_MODEL), F32),
        compiler_params=_params(("parallel", "parallel")),
        name="merge",
    )(x, mods, mods, mods, a, b, o_f, o_b, r, *consts)


def _ffn_kernel(x_ref, sh_ref, sc_ref, g2_ref, w1_ref, w2_ref, o_ref):
    x = x_ref[0]
    hb = _modulated(x, sh_ref[0], sc_ref[0]).astype(BF16)
    t = jnp.maximum(_dot(hb, w1_ref[...]), 0.0)
    f = _dot((t * t).astype(BF16), w2_ref[...])
    o_ref[0] = x + g2_ref[0] * f


def _ffn(x, mods, mod_row, lw, tm):
    B, N, _ = x.shape
    tok = pl.BlockSpec((1, tm, D_MODEL), lambda b, i: (b, i, 0))
    return pl.pallas_call(
        _ffn_kernel,
        grid=(B, N // tm),
        in_specs=[tok] + [_mod_spec(j, mod_row) for j in (3, 4, 5)]
                 + [_const_spec(lw["w_ff1"].shape), _const_spec(lw["w_ff2"].shape)],
        out_specs=tok,
        out_shape=jax.ShapeDtypeStruct((B, N, D_MODEL), F32),
        compiler_params=_params(("parallel", "parallel")),
        name="ffn",
    )(x, mods, mods, mods, lw["w_ff1"], lw["w_ff2"])


def _rope_tables(n_tokens):
    lane = np.arange(LANES)
    d = lane % HEAD_DIM
    use_col = (d // (HEAD_DIM // 2)) == 1
    second = ((d % (HEAD_DIM // 2)) // (HEAD_DIM // 4)) == 1
    f = d % (HEAD_DIM // 4)
    n_freq = HEAD_DIM // 4
    freqs = ROPE_BASE ** (-jnp.arange(n_freq, dtype=F32) / n_freq)
    t = jnp.arange(n_tokens)
    row = (t // GRID_W).astype(F32)
    col = (t % GRID_W).astype(F32)
    ang_row = row[:, None] * freqs[None, :]
    ang_col = col[:, None] * freqs[None, :]
    ang = jnp.where(jnp.asarray(use_col)[None, :], ang_col[:, f], ang_row[:, f])
    cos, sin = jnp.cos(ang), jnp.sin(ang)
    second = jnp.asarray(second)[None, :]
    return cos, jnp.where(second, sin, 0.0), jnp.where(second, 0.0, -sin)


def _identity_tables(n_tokens):
    return (jnp.ones((n_tokens, LANES), F32), jnp.zeros((n_tokens, LANES), F32),
            jnp.zeros((n_tokens, LANES), F32))


def _layer_weights(l, w_in, q_norm, k_norm, sgu_norm, w_sgu, b_sgu, w_gate_f, b_gate_f,
                   w_gate_b, b_gate_b, gla_norm, w_br, w_o, w_ff1, w_ff2):
    w = w_in[l]
    col = lambda i: w[:, _OFFS[i]:_OFFS[i + 1]]
    wq = col(2).reshape(D_MODEL, SWA_KV_HEADS, SWA_GROUP, HEAD_DIM).transpose(0, 2, 1, 3).reshape(D_MODEL, 512)
    w_z = jnp.zeros((D_MODEL, LANES), F32).at[:, :2 * GLA_RANK].set(w[:, _OFFS[9]:_OFFS[11]])
    n_dk = GLA_HEADS * GLA_DK
    w2 = jnp.zeros((LANES, 2 * n_dk), F32)
    w2 = w2.at[:GLA_RANK, :n_dk].set(w_gate_f[l]).at[GLA_RANK:2 * GLA_RANK, n_dk:].set(w_gate_b[l])
    b2 = jnp.concatenate([b_gate_f[l], b_gate_b[l]])[None, :]
    wbr1 = w_br[l, 1].reshape(SWA_KV_HEADS, SWA_GROUP, HEAD_DIM, D_MODEL).transpose(1, 0, 2, 3).reshape(512, D_MODEL)
    return {
        "w_uv": w[:, _OFFS[0]:_OFFS[2]].astype(BF16),
        "w_b": jnp.concatenate([wq, col(3), col(4)], axis=1).astype(BF16),
        "w_c": w[:, _OFFS[5]:_OFFS[9]].astype(BF16),
        "w_z": w_z.astype(BF16),
        "w_g": col(11).astype(BF16),
        "w2": w2.astype(BF16),
        "b2": b2,
        "sgu_norm": sgu_norm[l][None, :],
        "w_sgu": w_sgu[l].astype(BF16),
        "b_sgu": jnp.broadcast_to(b_sgu[l][:, :, None], (SGU_GROUPS, SGU_CHUNK, LANES)),
        "q_norm": jnp.tile(q_norm[l], 2)[None, :],
        "k_norm": jnp.tile(k_norm[l], 2)[None, :],
        "gla_norm": gla_norm[l][None, :],
        "w_br": jnp.stack([w_br[l, 0], wbr1, w_br[l, 2]]).astype(BF16),
        "w_o": w_o[l].astype(BF16),
        "w_ff1": w_ff1[l].astype(BF16),
        "w_ff2": w_ff2[l].astype(BF16),
    }


def kernel(x, c, ctx, c_ctx, w_ada, b_ada, w_in, q_norm, k_norm, sink, sgu_norm, w_sgu, b_sgu,
           w_gate_f, b_gate_f, w_gate_b, b_gate_b, gla_norm, w_br, w_o, w_ff1, w_ff2):
    B, N, _ = x.shape
    C = ctx.shape[1]
    depth = w_ada.shape[0]
    n_cond = 16
    tm_lat = min(TOKEN_BLOCK, N)
    tm_ctx = min(TOKEN_BLOCK, B * C)
    gla_tile = min(GLA_TILE, N)

    cond = jnp.zeros((n_cond, D_MODEL), F32).at[:B].set(c).at[B].set(c_ctx)
    mods = _adaln(cond, w_ada, b_ada)
    rope_lat = _rope_tables(N)
    rope_ctx = _identity_tables(B * C)
    zero_state = jnp.zeros((B,) + GLA_STATE_SHAPE, F32)

    flat = lambda t: t.reshape(1, B * C, t.shape[-1])
    per_batch = lambda t: t.reshape(B, C, t.shape[-1])
    ctx = flat(ctx)

    for l in range(depth):
        lw = _layer_weights(l, w_in, q_norm, k_norm, sgu_norm, w_sgu, b_sgu, w_gate_f, b_gate_f,
                            w_gate_b, b_gate_b, gla_norm, w_br, w_o, w_ff1, w_ff2)
        m = mods[l].reshape(n_cond * 6, 1, D_MODEL)

        (ca, cq, ck, cv, ckq, cvc, cr, cg2) = _proj(ctx, m, B, rope_ctx, lw, tm_ctx)
        ck, cv = per_batch(ck), per_batch(cv)
        oc_f, oc_b, s_ctx = _gla(per_batch(ckq), per_batch(cvc), per_batch(cg2), zero_state, min(gla_tile, C))

        (a, q, k, v, kq, vc, r, g2) = _proj(x, m, None, rope_lat, lw, tm_lat)
        b_out = _attn(q, k, v, ck, cv, sink[l], local=True)
        o_f, o_b, _ = _gla(kq, vc, g2, s_ctx, gla_tile)
        x_new = _merge(x, m, None, a, b_out, o_f, o_b, r, lw, tm_lat)
        x_new = _ffn(x_new, m, None, lw, tm_lat)

        if l < depth - 1:
            cb_out = _attn(per_batch(cq), None, None, ck, cv, sink[l], local=False)
            ctx_new = _merge(ctx, m, B, ca, flat(cb_out), flat(oc_f), flat(oc_b), cr, lw, tm_ctx)
            ctx = _ffn(ctx_new, m, B, lw, tm_ctx)
        x = x_new

    return x
```

```python
import functools
import itertools

import jax
import jax.numpy as jnp
import numpy as np
from jax import lax
from jax.experimental import pallas as pl
from jax.experimental.pallas import tpu as pltpu

D_MODEL = 1024
GRID_W = 64
HEAD_DIM = 64
BRANCH_WIDTH = 512
N_BRANCH = 3
SGU_GROUPS = 4
SGU_CHUNK = 128
SWA_Q_HEADS = 8
SWA_KV_HEADS = 2
SWA_GROUP = 4
SWA_WINDOW = 128
SWA_BLOCK = 128
ROPE_BASE = 10000.0
GLA_HEADS = 4
GLA_DK = 64
GLA_DV = 128
GLA_RANK = 16
GLA_NORMALIZER = 16.0
GLA_CHUNK = 64
FFN_DIM = 4 * D_MODEL
EPS = 1e-6

LANES = 128
VMEM_LIMIT = 56 * 1024 * 1024
NEG_BIG = -1e30
LOG2E = 1.4426950408889634
ADALN_COLS = 3072
TOKEN_BLOCK = 1024
PROJ_SUBTILE = 256
MERGE_SUBTILE = 256
ATTN_Q_BLOCKS = 16
GLA_TILE = 2048
GLA_CUM_ROWS = 256
GLA_SEQS_PER_STEP = 1
GLA_GROUP_STEPS = 1
GLA_STATE_SHAPE = (2, GLA_HEADS // 2, 2 * GLA_DV, LANES)
BF16 = jnp.bfloat16
F32 = jnp.float32

Q_END = SWA_Q_HEADS * HEAD_DIM
K_END = Q_END + SWA_KV_HEADS * HEAD_DIM

_SPLITS = (512, 512, 512, 128, 128, 256, 256, 512, 512, 16, 16, 3072)
_OFFS = np.concatenate([[0], np.cumsum(_SPLITS)]).tolist()


def _dot(a, b):
    return jnp.dot(a, b, preferred_element_type=F32)


def _dot_nt(a, b):
    return lax.dot_general(a, b, (((1,), (1,)), ((), ())), preferred_element_type=F32)


def _dot_tn(a, b):
    return lax.dot_general(a, b, (((0,), (0,)), ((), ())), preferred_element_type=F32)


def _const_spec(shape):
    nd = len(shape)
    return pl.BlockSpec(shape, lambda *_: (0,) * nd, pipeline_mode=pl.Buffered(1))


def _params(sem):
    return pltpu.CompilerParams(dimension_semantics=sem, vmem_limit_bytes=VMEM_LIMIT)


def _adaln_kernel(cond_ref, w_ref, b_ref, o_ref):
    cond = cond_ref[...]
    s = (cond * jax.nn.sigmoid(cond)).astype(BF16)
    o_ref[0] = _dot(s, w_ref[0].astype(BF16)) + b_ref[0]


def _adaln(cond, w_ada, b_ada):
    depth = w_ada.shape[0]
    rows = cond.shape[0]
    tn = ADALN_COLS
    return pl.pallas_call(
        _adaln_kernel,
        grid=(depth, 6 * D_MODEL // tn),
        in_specs=[
            pl.BlockSpec((rows, D_MODEL), lambda l, j: (0, 0)),
            pl.BlockSpec((1, D_MODEL, tn), lambda l, j: (l, 0, j)),
            pl.BlockSpec((1, 1, tn), lambda l, j: (l, 0, j)),
        ],
        out_specs=pl.BlockSpec((1, rows, tn), lambda l, j: (l, 0, j)),
        out_shape=jax.ShapeDtypeStruct((depth, rows, 6 * D_MODEL), F32),
        compiler_params=_params(("arbitrary", "arbitrary")),
        name="adaln",
    )(cond, w_ada, b_ada.reshape(depth, 1, 6 * D_MODEL))


def _sub_tiles(n_rows, size=PROJ_SUBTILE):
    return [slice(r0, r0 + size) for r0 in range(0, n_rows, size)]


def _modulated(x, shift, scale):
    ms = jnp.mean(x * x, axis=-1, keepdims=True)
    return x * lax.rsqrt(ms + EPS) * (1.0 + scale) + shift


def _proj_kernel(x_ref, sh_ref, sc_ref, cos_ref, sa_ref, sb_ref,
                 wuv_ref, wb_ref, wc_ref, wz_ref, w2_ref, b2_ref,
                 sgun_ref, ws_ref, bs_ref, qn_ref, kn_ref,
                 a_ref, q_ref, k_ref, v_ref, kq_ref, vc_ref, r_ref, g2_ref):
    ri = lax.broadcasted_iota(jnp.int32, (2 * LANES, 2 * LANES), 0) // HEAD_DIM
    ci = lax.broadcasted_iota(jnp.int32, (2 * LANES, 2 * LANES), 1) // HEAD_DIM
    group_ones = jnp.where(ri == ci, 1.0, 0.0).astype(BF16)

    tiles = _sub_tiles(x_ref.shape[1])

    hbs = {rows.start: _modulated(x_ref[0, rows], sh_ref[0], sc_ref[0]).astype(BF16) for rows in tiles}
    zs = {0: _dot(hbs[0], wz_ref[...]).astype(BF16)}

    def big(rows):
        hb = hbs[rows.start]
        t = dict(uv=_dot(hb, wuv_ref[...]))
        nxt = rows.stop
        if nxt in hbs:
            zs[nxt] = _dot(hbs[nxt], wz_ref[...]).astype(BF16)
        t["bq"] = _dot(hb, wb_ref[...])
        t["z"] = zs[rows.start]
        return t

    def mixer_c_projection(rows):
        cp = _dot(hbs[rows.start], wc_ref[...])
        n_q = GLA_HEADS * GLA_DK
        v_end = 2 * n_q + BRANCH_WIDTH
        kq_ref[0, rows, :n_q] = cp[:, :n_q] * GLA_DK ** -0.5
        kq_ref[0, rows, n_q:] = cp[:, n_q:2 * n_q]
        vc_ref[0, rows] = cp[:, 2 * n_q:v_end].astype(vc_ref.dtype)
        r_ref[0, rows] = cp[:, v_end:].astype(r_ref.dtype)

    def prepare(rows, t):
        bq, uv = t["bq"], t["uv"]
        v_ref[0, rows] = bq[:, K_END:].astype(v_ref.dtype)
        t["sq"] = [(xs * xs).astype(BF16) for xs in (bq[:, :2 * LANES], bq[:, 2 * LANES:Q_END], bq[:, Q_END:])]
        t["u"] = jax.nn.gelu(uv[:, :BRANCH_WIDTH])
        va = jax.nn.gelu(uv[:, BRANCH_WIDTH:])
        t["vg"] = []
        for g in range(SGU_GROUPS):
            lanes = slice(g * LANES, (g + 1) * LANES)
            vg = va[:, lanes]
            vg = vg * lax.rsqrt(jnp.mean(vg * vg, axis=-1, keepdims=True) + EPS) * sgun_ref[:, lanes]
            t["vg"].append(vg.astype(BF16))

    def small(rows, t):
        bq = t["bq"]
        cos, sa, sb = cos_ref[rows], sa_ref[rows], sb_ref[rows]

        def rope(y):
            return y * cos + pltpu.roll(y, HEAD_DIM // 4, 1) * sa + pltpu.roll(y, LANES - HEAD_DIM // 4, 1) * sb

        logit = _dot(t["z"], w2_ref[...]) + b2_ref[...]
        log_sig = jnp.minimum(logit, 0.0) - jnp.log(1.0 + jnp.exp(-jnp.abs(logit)))
        g2_ref[0, rows] = log_sig * (1.0 / GLA_NORMALIZER)
        inv = [lax.rsqrt(_dot(sq, group_ones) * (1.0 / HEAD_DIM) + EPS) for sq in t["sq"]]
        for s in range(BRANCH_WIDTH // LANES):
            lanes = slice(s * LANES, (s + 1) * LANES)
            qn = bq[:, lanes] * inv[s // 2][:, (s % 2) * LANES:(s % 2 + 1) * LANES] * qn_ref[...]
            q_ref[0, rows, lanes] = (rope(qn) * (HEAD_DIM ** -0.5 * LOG2E)).astype(q_ref.dtype)
        k_ref[0, rows] = rope(bq[:, Q_END:K_END] * inv[2][:, :LANES] * kn_ref[...]).astype(k_ref.dtype)
        n_chunk = (rows.stop - rows.start) // SGU_CHUNK
        for g in range(SGU_GROUPS):
            lanes = slice(g * LANES, (g + 1) * LANES)
            vg = jnp.concatenate([t["vg"][g][c * SGU_CHUNK:(c + 1) * SGU_CHUNK] for c in range(n_chunk)], axis=1)
            s_all = _dot(ws_ref[g], vg)
            for c in range(n_chunk):
                cr = slice(c * SGU_CHUNK, (c + 1) * SGU_CHUNK)
                out_rows = slice(rows.start + c * SGU_CHUNK, rows.start + (c + 1) * SGU_CHUNK)
                s = s_all[:, c * LANES:(c + 1) * LANES] + bs_ref[g]
                a_ref[0, out_rows, lanes] = (t["u"][cr, lanes] * s).astype(a_ref.dtype)

    prev = None
    for rows in tiles:
        cur = big(rows)
        if rows is not tiles[-1]:
            mixer_c_projection(rows)
        if prev is not None:
            small(*prev)
        prepare(rows, cur)
        prev = (rows, cur)
    small(*prev)
    mixer_c_projection(tiles[-1])


def _mod_spec(j, mod_row):
    if mod_row is None:
        return pl.BlockSpec((1, 1, D_MODEL), lambda b, i: (b * 6 + j, 0, 0))
    return pl.BlockSpec((1, 1, D_MODEL), lambda b, i: (mod_row * 6 + j, 0, 0))


def _proj(x, mods, mod_row, rope, lw, tm):
    B, N, _ = x.shape
    cos, sa, sb = rope
    tok = lambda w: pl.BlockSpec((1, tm, w), lambda b, i: (b, i, 0))
    tab = pl.BlockSpec((tm, LANES), lambda b, i: (i, 0))
    consts = [lw["w_uv"], lw["w_b"], lw["w_c"], lw["w_z"], lw["w2"], lw["b2"],
              lw["sgu_norm"], lw["w_sgu"], lw["b_sgu"], lw["q_norm"], lw["k_norm"]]
    out_w = [(512, BF16), (512, BF16), (128, BF16), (128, BF16), (512, F32), (512, BF16),
             (512, BF16), (512, F32)]
    return pl.pallas_call(
        _proj_kernel,
        grid=(B, N // tm),
        in_specs=[tok(D_MODEL), _mod_spec(0, mod_row), _mod_spec(1, mod_row), tab, tab, tab]
                 + [_const_spec(c.shape) for c in consts],
        out_specs=[tok(w) for w, _ in out_w],
        out_shape=[jax.ShapeDtypeStruct((B, N, w), dt) for w, dt in out_w],
        compiler_params=_params(("parallel", "parallel")),
        name="proj",
    )(x, mods, mods, cos, sa, sb, *consts)


def _attn_kernel(sink_ref, q_ref, *refs, local):
    blk = SWA_BLOCK
    n_q = q_ref.shape[1] // blk
    n_slab = BRANCH_WIDTH // LANES
    if local:
        band_ref, kp_ref, kc_ref, kn_ref, vp_ref, vc_ref, vn_ref, kx_ref, vx_ref, o_ref = refs
        k_loc = [kp_ref[0]] + [kc_ref[0, j * blk:(j + 1) * blk] for j in range(n_q)] + [kn_ref[0]]
        v_loc = [vp_ref[0]] + [vc_ref[0, j * blk:(j + 1) * blk] for j in range(n_q)] + [vn_ref[0]]
        keys = [jnp.concatenate(k_loc[j:j + 3] + [kx_ref[0]], axis=0) for j in range(n_q)]
        vals = [jnp.concatenate(v_loc[j:j + 3] + [vx_ref[0]], axis=0) for j in range(n_q)]
        i = pl.program_id(1)
        no_prev = jnp.where(i == 0, NEG_BIG, 0.0)
        no_next = jnp.where(i == pl.num_programs(1) - 1, NEG_BIG, 0.0)
        bias_prev = [band_ref[:, :blk] + no_prev if j == 0 else band_ref[:, :blk] for j in range(n_q)]
        bias_next = [band_ref[:, blk:] + no_next if j == n_q - 1 else band_ref[:, blk:] for j in range(n_q)]
    else:
        kx_ref, vx_ref, o_ref = refs
        keys, vals = [kx_ref[0]] * n_q, [vx_ref[0]] * n_q
    head0_lanes = lax.broadcasted_iota(jnp.int32, (blk, LANES), 1) < HEAD_DIM
    val_lane = lax.broadcasted_iota(jnp.int32, vals[0].shape, 1)

    scores = {}

    def score_block(j):
        rows = slice(j * blk, (j + 1) * blk)
        for h in range(SWA_KV_HEADS):
            own = head0_lanes if h == 0 else ~head0_lanes
            qh = [jnp.where(own, q_ref[0, rows, s * LANES:(s + 1) * LANES], 0.0).astype(BF16)
                  for s in range(n_slab)]
            scores[j, h] = _dot_nt(jnp.concatenate(qh, axis=0), keys[j])

    score_block(0)
    for j in range(n_q):
        if j + 1 < n_q:
            score_block(j + 1)
        outs = {}
        for h in range(SWA_KV_HEADS):
            ps, sink_terms = [], []
            for s in range(n_slab):
                sc = scores[j, h][s * blk:(s + 1) * blk]
                if local:
                    sc = jnp.concatenate([sc[:, :blk] + bias_prev[j], sc[:, blk:2 * blk],
                                          sc[:, 2 * blk:3 * blk] + bias_next[j], sc[:, 3 * blk:]], axis=1)
                sink = sink_ref[h * SWA_GROUP + s] * LOG2E
                m = jnp.maximum(jnp.max(sc, axis=-1, keepdims=True), sink)
                ps.append(jnp.exp2(sc - m).astype(BF16))
                sink_terms.append(jnp.exp2(sink - m))
            own = (val_lane < HEAD_DIM) if h == 0 else (val_lane >= HEAD_DIM)
            pv = _dot(jnp.concatenate(ps, axis=0), jnp.where(own, vals[j], jnp.ones_like(vals[j])))
            for s in range(n_slab):
                pv_s = pv[s * blk:(s + 1) * blk]
                denom = pltpu.roll(pv_s, HEAD_DIM, 1) + sink_terms[s]
                outs[s, h] = pv_s / denom
        for s in range(n_slab):
            o_ref[0, j * blk:(j + 1) * blk, s * LANES:(s + 1) * LANES] = jnp.where(
                head0_lanes, outs[s, 0], outs[s, 1]).astype(o_ref.dtype)


def _band_bias(blk):
    qi = np.arange(blk)[:, None]
    kj = np.arange(blk)[None, :]
    prev_ok = np.abs(kj - blk - qi) <= SWA_WINDOW
    next_ok = np.abs(kj + blk - qi) <= SWA_WINDOW
    return jnp.asarray(np.where(np.concatenate([prev_ok, next_ok], axis=1), 0.0, NEG_BIG), F32)


def _attn(q, k, v, k_ctx, v_ctx, sink, local):
    B, N, _ = q.shape
    C = k_ctx.shape[1]
    blk = SWA_BLOCK
    nb = N // blk
    n_q = min(ATTN_Q_BLOCKS, nb)
    steps = nb // n_q
    qspec = pl.BlockSpec((1, n_q * blk, BRANCH_WIDTH), lambda b, i: (b, i, 0))
    ctx_spec = pl.BlockSpec((1, C, LANES), lambda b, i: (b, 0, 0))
    smem = pl.BlockSpec(memory_space=pltpu.SMEM)
    if local:
        prev = pl.BlockSpec((1, blk, LANES), lambda b, i: (b, jnp.maximum(n_q * i - 1, 0), 0))
        cur = pl.BlockSpec((1, n_q * blk, LANES), lambda b, i: (b, i, 0))
        nxt = pl.BlockSpec((1, blk, LANES), lambda b, i: (b, jnp.minimum(n_q * (i + 1), nb - 1), 0))
        band = _band_bias(blk)
        in_specs = [smem, qspec, _const_spec(band.shape), prev, cur, nxt, prev, cur, nxt, ctx_spec, ctx_spec]
        args = (sink, q, band, k, k, k, v, v, v, k_ctx, v_ctx)
    else:
        in_specs = [smem, qspec, ctx_spec, ctx_spec]
        args = (sink, q, k_ctx, v_ctx)
    return pl.pallas_call(
        functools.partial(_attn_kernel, local=local),
        grid=(B, steps),
        in_specs=in_specs,
        out_specs=qspec,
        out_shape=jax.ShapeDtypeStruct((B, N, BRANCH_WIDTH), BF16),
        compiler_params=_params(("parallel", "parallel")),
        name="attn_local" if local else "attn_ctx",
    )(*args)


def _gla_kernel(kqf_ref, vf_ref, gf_ref, kqb_ref, vb_ref, gb_ref, s0_ref,
                of_ref, ob_ref, sfin_ref, st_ref):
    i = pl.program_id(1)
    T = kqf_ref.shape[1]
    n_chunk = T // GLA_CHUNK

    n_seq = kqf_ref.shape[0]

    @pl.when(i == 0)
    def _():
        st_ref[...] = s0_ref[...]

    cum_rows = min(T, GLA_CUM_ROWS)
    ri = lax.broadcasted_iota(jnp.int32, (cum_rows, cum_rows), 0)
    ci = lax.broadcasted_iota(jnp.int32, (cum_rows, cum_rows), 1)
    same_chunk = (ri // GLA_CHUNK) == (ci // GLA_CHUNK)
    li = lax.broadcasted_iota(jnp.int32, (GLA_CHUNK, 2 * GLA_CHUNK), 0)
    mi = lax.broadcasted_iota(jnp.int32, (GLA_CHUNK, 2 * GLA_CHUNK), 1) % GLA_CHUNK
    first_head_lanes = lax.broadcasted_iota(jnp.int32, (GLA_CHUNK, LANES), 1) < GLA_DK
    own_block = ((lax.broadcasted_iota(jnp.int32, (2 * GLA_DV, LANES), 0) < GLA_DV)
                 == (lax.broadcasted_iota(jnp.int32, (2 * GLA_DV, LANES), 1) < GLA_DK))

    dirs = ((kqf_ref, vf_ref, gf_ref, of_ref), (kqb_ref, vb_ref, gb_ref, ob_ref))
    cums = {}
    for d, (kq_ref, v_ref, g_ref, o_ref) in enumerate(dirs):
        tri_full = jnp.where(same_chunk & ((ci >= ri) if d == 1 else (ci <= ri)), 1.0, 0.0).astype(BF16)
        for b in range(n_seq):
            cums[b, d] = jnp.concatenate([_split_dot_left(tri_full, g_ref[b, r0:r0 + cum_rows])
                                          for r0 in range(0, T, cum_rows)], axis=0)

    def chunk_of(d, step):
        return n_chunk - 1 - step if d == 1 else step

    def operands(b, d, p, c, anchor):
        kq_ref = dirs[d][0]
        lanes = slice(p * LANES, (p + 1) * LANES)
        k_lanes = slice(GLA_HEADS * GLA_DK + p * LANES, GLA_HEADS * GLA_DK + (p + 1) * LANES)
        rows = slice(c * GLA_CHUNK, (c + 1) * GLA_CHUNK)
        ref_row = GLA_CHUNK // 2 - 1 if d == 1 else GLA_CHUNK // 2
        tot_row = 0 if d == 1 else GLA_CHUNK - 1
        bc = cums[b, d][rows, lanes]
        if anchor is not None:
            bc = bc + anchor
        q2 = kq_ref[b, rows, lanes]
        k2 = kq_ref[b, rows, k_lanes]
        b_ref = bc[ref_row:ref_row + 1]
        b_tot = bc[tot_row:tot_row + 1]
        ke = k2 * jnp.exp(b_ref - bc)
        return dict(
            qe=(q2 * jnp.exp(bc - b_ref)).astype(BF16),
            q_in=(q2 * jnp.exp(bc)).astype(BF16),
            k_out=(k2 * jnp.exp(b_tot - bc)).astype(BF16),
            decay=jnp.exp(b_tot),
            ke=jnp.concatenate([jnp.where(first_head_lanes, ke, 0.0),
                                jnp.where(first_head_lanes, 0.0, ke)], axis=0).astype(BF16))

    ops = {}
    anchors = {}

    def score_group(gi, steps):
        anchor = anchors.get(gi - 2)
        for step in steps:
            for b in range(n_seq):
                for d in range(2):
                    for p in range(GLA_HEADS // 2):
                        u = operands(b, d, p, chunk_of(d, step), anchor)
                        keep = (mi >= li) if d == 1 else (mi <= li)
                        scores = _dot_nt(u["qe"], u["ke"])
                        u["pm"] = jnp.where(keep, scores, 0.0).astype(BF16)
                        ops[b, d, p, step] = u
        anchors[gi] = jnp.minimum(jnp.abs(scores[:1]), 0.0)

    zeros_v = jnp.zeros((GLA_CHUNK, GLA_DV), BF16)

    def scan_group(steps):
        for step, b, (d, (kq_ref, v_ref, g_ref, o_ref)), p in itertools.product(
                steps, range(n_seq), enumerate(dirs), range(GLA_HEADS // 2)):
            c = chunk_of(d, step)
            rows = slice(c * GLA_CHUNK, (c + 1) * GLA_CHUNK)
            lanes = slice(p * 2 * GLA_DV, (p + 1) * 2 * GLA_DV)
            u = ops.pop((b, d, p, step))
            v2 = v_ref[b, rows, lanes]
            v_diag = jnp.concatenate(
                [jnp.concatenate([v2[:, :GLA_DV], zeros_v], axis=1),
                 jnp.concatenate([zeros_v, v2[:, GLA_DV:]], axis=1)], axis=0)
            st = st_ref[b, d, p]
            o = _dot(u["pm"], v_diag) + _dot_nt(u["q_in"], st.astype(BF16))
            o_ref[b, rows, lanes] = o.astype(o_ref.dtype)
            st_ref[b, d, p] = st * u["decay"] + jnp.where(own_block, _dot_tn(v2, u["k_out"]), 0.0)

    groups = [range(s, min(s + GLA_GROUP_STEPS, n_chunk)) for s in range(0, n_chunk, GLA_GROUP_STEPS)]
    score_group(0, groups[0])
    for gi, steps in enumerate(groups):
        if gi + 1 < len(groups):
            score_group(gi + 1, groups[gi + 1])
        scan_group(steps)

    @pl.when(i == pl.num_programs(1) - 1)
    def _():
        sfin_ref[...] = st_ref[...]


def _split_dot_left(m, a):
    hi = a.astype(BF16)
    lo = (a - hi.astype(F32)).astype(BF16)
    return _dot(m, hi) + _dot(m, lo)


def _gla(kq, v, g2, s0, T):
    B, N, _ = kq.shape
    nt = N // T
    nb = GLA_SEQS_PER_STEP
    fwd = lambda w: pl.BlockSpec((nb, T, w), lambda b, i: (b, i, 0))
    bwd = lambda w: pl.BlockSpec((nb, T, w), lambda b, i: (b, nt - 1 - i, 0))
    g_fwd = pl.BlockSpec((nb, T, 256), lambda b, i: (b, i, 0))
    g_bwd = pl.BlockSpec((nb, T, 256), lambda b, i: (b, nt - 1 - i, 1))
    st_shape = GLA_STATE_SHAPE
    st_spec = pl.BlockSpec((nb,) + st_shape, lambda b, i: (b, 0, 0, 0, 0))
    return pl.pallas_call(
        _gla_kernel,
        grid=(B // nb, nt),
        in_specs=[fwd(512), fwd(512), g_fwd, bwd(512), bwd(512), g_bwd, st_spec],
        out_specs=[fwd(512), bwd(512), st_spec],
        out_shape=[jax.ShapeDtypeStruct((B, N, 512), BF16), jax.ShapeDtypeStruct((B, N, 512), BF16),
                   jax.ShapeDtypeStruct((B,) + st_shape, F32)],
        scratch_shapes=[pltpu.VMEM((nb,) + st_shape, F32)],
        compiler_params=_params(("parallel", "arbitrary")),
        name="gla",
    )(kq, v, g2, kq, v, g2, s0)


def _merge_kernel(x_ref, sh_ref, sc_ref, g1_ref, a_ref, b_ref, of_ref, ob_ref, r_ref,
                  gn_ref, wg_ref, wbr_ref, wo_ref, o_ref):
    def gated_mix(rows):
        hb = _modulated(x_ref[0, rows], sh_ref[0], sc_ref[0]).astype(BF16)
        o = of_ref[0, rows].astype(F32) + ob_ref[0, rows].astype(F32)
        cs = []
        for h in range(GLA_HEADS):
            oh = o[:, h * LANES:(h + 1) * LANES]
            cs.append(oh * lax.rsqrt(jnp.mean(oh * oh, axis=-1, keepdims=True) + EPS) * gn_ref[...])
        r = r_ref[0, rows].astype(F32)
        c_out = (jnp.concatenate(cs, axis=1) * (r * jax.nn.sigmoid(r))).astype(BF16)
        branches = (a_ref[0, rows], b_ref[0, rows], c_out)
        gates = [jax.nn.sigmoid(_dot(hb, wg_ref[:, k * D_MODEL:(k + 1) * D_MODEL])) for k in range(N_BRANCH)]
        mixed = None
        for k in range(N_BRANCH):
            term = gates[k] * _dot(branches[k], wbr_ref[k])
            mixed = term if mixed is None else mixed + term
        return mixed.astype(BF16)

    def project_out(rows, mixed):
        o_ref[0, rows] = x_ref[0, rows] + g1_ref[0] * _dot(mixed, wo_ref[...])

    prev = None
    for rows in _sub_tiles(x_ref.shape[1], MERGE_SUBTILE):
        mixed = gated_mix(rows)
        if prev is not None:
            project_out(*prev)
        prev = (rows, mixed)
    project_out(*prev)


def _merge(x, mods, mod_row, a, b, o_f, o_b, r, lw, tm):
    B, N, _ = x.shape
    tok = lambda w: pl.BlockSpec((1, tm, w), lambda b_, i: (b_, i, 0))
    consts = [lw["gla_norm"], lw["w_g"], lw["w_br"], lw["w_o"]]
    return pl.pallas_call(
        _merge_kernel,
        grid=(B, N // tm),
        in_specs=[tok(D_MODEL)] + [_mod_spec(j, mod_row) for j in range(3)] + [tok(512)] * 5
                 + [_const_spec(c.shape) for c in consts],
        out_specs=tok(D_MODEL),
        out_shape=jax.ShapeDtypeStruct((B, N, D_MODEL), F32),
        compiler_params=_params(("parallel", "parallel")),
        name="merge",
    )(x, mods, mods, mods, a, b, o_f, o_b, r, *consts)


def _ffn_kernel(x_ref, sh_ref, sc_ref, g2_ref, w1_ref, w2_ref, o_ref):
    x = x_ref[0]
    hb = _modulated(x, sh_ref[0], sc_ref[0]).astype(BF16)
    t = jnp.maximum(_dot(hb, w1_ref[...]), 0.0)
    f = _dot((t * t).astype(BF16), w2_ref[...])
    o_ref[0] = x + g2_ref[0] * f


def _ffn(x, mods, mod_row, lw, tm):
    B, N, _ = x.shape
    tok = pl.BlockSpec((1, tm, D_MODEL), lambda b, i: (b, i, 0))
    return pl.pallas_call(
        _ffn_kernel,
        grid=(B, N // tm),
        in_specs=[tok] + [_mod_spec(j, mod_row) for j in (3, 4, 5)]
                 + [_const_spec(lw["w_ff1"].shape), _const_spec(lw["w_ff2"].shape)],
        out_specs=tok,
        out_shape=jax.ShapeDtypeStruct((B, N, D_MODEL), F32),
        compiler_params=_params(("parallel", "parallel")),
        name="ffn",
    )(x, mods, mods, mods, lw["w_ff1"], lw["w_ff2"])


def _rope_tables(n_tokens):
    lane = np.arange(LANES)
    d = lane % HEAD_DIM
    use_col = (d // (HEAD_DIM // 2)) == 1
    second = ((d % (HEAD_DIM // 2)) // (HEAD_DIM // 4)) == 1
    f = d % (HEAD_DIM // 4)
    n_freq = HEAD_DIM // 4
    freqs = ROPE_BASE ** (-jnp.arange(n_freq, dtype=F32) / n_freq)
    t = jnp.arange(n_tokens)
    row = (t // GRID_W).astype(F32)
    col = (t % GRID_W).astype(F32)
    ang_row = row[:, None] * freqs[None, :]
    ang_col = col[:, None] * freqs[None, :]
    ang = jnp.where(jnp.asarray(use_col)[None, :], ang_col[:, f], ang_row[:, f])
    cos, sin = jnp.cos(ang), jnp.sin(ang)
    second = jnp.asarray(second)[None, :]
    return cos, jnp.where(second, sin, 0.0), jnp.where(second, 0.0, -sin)


def _identity_tables(n_tokens):
    return (jnp.ones((n_tokens, LANES), F32), jnp.zeros((n_tokens, LANES), F32),
            jnp.zeros((n_tokens, LANES), F32))


def _layer_weights(l, w_in, q_norm, k_norm, sgu_norm, w_sgu, b_sgu, w_gate_f, b_gate_f,
                   w_gate_b, b_gate_b, gla_norm, w_br, w_o, w_ff1, w_ff2):
    w = w_in[l]
    col = lambda i: w[:, _OFFS[i]:_OFFS[i + 1]]
    wq = col(2).reshape(D_MODEL, SWA_KV_HEADS, SWA_GROUP, HEAD_DIM).transpose(0, 2, 1, 3).reshape(D_MODEL, 512)
    w_z = jnp.zeros((D_MODEL, LANES), F32).at[:, :2 * GLA_RANK].set(w[:, _OFFS[9]:_OFFS[11]])
    n_dk = GLA_HEADS * GLA_DK
    w2 = jnp.zeros((LANES, 2 * n_dk), F32)
    w2 = w2.at[:GLA_RANK, :n_dk].set(w_gate_f[l]).at[GLA_RANK:2 * GLA_RANK, n_dk:].set(w_gate_b[l])
    b2 = jnp.concatenate([b_gate_f[l], b_gate_b[l]])[None, :]
    wbr1 = w_br[l, 1].reshape(SWA_KV_HEADS, SWA_GROUP, HEAD_DIM, D_MODEL).transpose(1, 0, 2, 3).reshape(512, D_MODEL)
    return {
        "w_uv": w[:, _OFFS[0]:_OFFS[2]].astype(BF16),
        "w_b": jnp.concatenate([wq, col(3), col(4)], axis=1).astype(BF16),
        "w_c": w[:, _OFFS[5]:_OFFS[9]].astype(BF16),
        "w_z": w_z.astype(BF16),
        "w_g": col(11).astype(BF16),
        "w2": w2.astype(BF16),
        "b2": b2,
        "sgu_norm": sgu_norm[l][None, :],
        "w_sgu": w_sgu[l].astype(BF16),
        "b_sgu": jnp.broadcast_to(b_sgu[l][:, :, None], (SGU_GROUPS, SGU_CHUNK, LANES)),
        "q_norm": jnp.tile(q_norm[l], 2)[None, :],
        "k_norm": jnp.tile(k_norm[l], 2)[None, :],
        "gla_norm": gla_norm[l][None, :],
        "w_br": jnp.stack([w_br[l, 0], wbr1, w_br[l, 2]]).astype(BF16),
        "w_o": w_o[l].astype(BF16),
        "w_ff1": w_ff1[l].astype(BF16),
        "w_ff2": w_ff2[l].astype(BF16),
    }


def kernel(x, c, ctx, c_ctx, w_ada, b_ada, w_in, q_norm, k_norm, sink, sgu_norm, w_sgu, b_sgu,
           w_gate_f, b_gate_f, w_gate_b, b_gate_b, gla_norm, w_br, w_o, w_ff1, w_ff2):
    B, N, _ = x.shape
    C = ctx.shape[1]
    depth = w_ada.shape[0]
    n_cond = 16
    tm_lat = min(TOKEN_BLOCK, N)
    tm_ctx = min(TOKEN_BLOCK, B * C)
    gla_tile = min(GLA_TILE, N)

    cond = jnp.zeros((n_cond, D_MODEL), F32).at[:B].set(c).at[B].set(c_ctx)
    mods = _adaln(cond, w_ada, b_ada)
    rope_lat = _rope_tables(N)
    rope_ctx = _identity_tables(B * C)
    zero_state = jnp.zeros((B,) + GLA_STATE_SHAPE, F32)

    flat = lambda t: t.reshape(1, B * C, t.shape[-1])
    per_batch = lambda t: t.reshape(B, C, t.shape[-1])
    ctx = flat(ctx)

    for l in range(depth):
        lw = _layer_weights(l, w_in, q_norm, k_norm, sgu_norm, w_sgu, b_sgu, w_gate_f, b_gate_f,
                            w_gate_b, b_gate_b, gla_norm, w_br, w_o, w_ff1, w_ff2)
        m = mods[l].reshape(n_cond * 6, 1, D_MODEL)

        (ca, cq, ck, cv, ckq, cvc, cr, cg2) = _proj(ctx, m, B, rope_ctx, lw, tm_ctx)
        ck, cv = per_batch(ck), per_batch(cv)
        oc_f, oc_b, s_ctx = _gla(per_batch(ckq), per_batch(cvc), per_batch(cg2), zero_state, min(gla_tile, C))

        (a, q, k, v, kq, vc, r, g2) = _proj(x, m, None, rope_lat, lw, tm_lat)
        b_out = _attn(q, k, v, ck, cv, sink[l], local=True)
        o_f, o_b, _ = _gla(kq, vc, g2, s_ctx, gla_tile)
        x_new = _merge(x, m, None, a, b_out, o_f, o_b, r, lw, tm_lat)
        x_new = _ffn(x_new, m, None, lw, tm_lat)

        if l < depth - 1:
            cb_out = _attn(per_batch(cq), None, None, ck, cv, sink[l], local=False)
            ctx_new = _merge(ctx, m, B, ca, flat(cb_out), flat(oc_f), flat(oc_b), cr, lw, tm_ctx)
            ctx = _ffn(ctx_new, m, B, lw, tm_ctx)
        x = x_new

    return x
```

```python
import functools
import itertools

import jax
import jax.numpy as jnp
import numpy as np
from jax import lax
from jax.experimental import pallas as pl
from jax.experimental.pallas import tpu as pltpu

D_MODEL = 1024
GRID_W = 64
HEAD_DIM = 64
BRANCH_WIDTH = 512
N_BRANCH = 3
SGU_GROUPS = 4
SGU_CHUNK = 128
SWA_Q_HEADS = 8
SWA_KV_HEADS = 2
SWA_GROUP = 4
SWA_WINDOW = 128
SWA_BLOCK = 128
ROPE_BASE = 10000.0
GLA_HEADS = 4
GLA_DK = 64
GLA_DV = 128
GLA_RANK = 16
GLA_NORMALIZER = 16.0
GLA_CHUNK = 64
FFN_DIM = 4 * D_MODEL
EPS = 1e-6

LANES = 128
VMEM_LIMIT = 56 * 1024 * 1024
NEG_BIG = -1e30
LOG2E = 1.4426950408889634
ADALN_COLS = 3072
TOKEN_BLOCK = 1024
PROJ_SUBTILE = 256
MERGE_SUBTILE = 256
ATTN_Q_BLOCKS = 16
GLA_TILE = 2048
GLA_CUM_ROWS = 256
GLA_SEQS_PER_STEP = 1
GLA_GROUP_STEPS = 1
GLA_STATE_SHAPE = (2, GLA_HEADS // 2, 2 * GLA_DV, LANES)
BF16 = jnp.bfloat16
F32 = jnp.float32

Q_END = SWA_Q_HEADS * HEAD_DIM
K_END = Q_END + SWA_KV_HEADS * HEAD_DIM

_SPLITS = (512, 512, 512, 128, 128, 256, 256, 512, 512, 16, 16, 3072)
_OFFS = np.concatenate([[0], np.cumsum(_SPLITS)]).tolist()


def _dot(a, b):
    return jnp.dot(a, b, preferred_element_type=F32)


def _dot_nt(a, b):
    return lax.dot_general(a, b, (((1,), (1,)), ((), ())), preferred_element_type=F32)


def _dot_tn(a, b):
    return lax.dot_general(a, b, (((0,), (0,)), ((), ())), preferred_element_type=F32)


def _const_spec(shape):
    nd = len(shape)
    return pl.BlockSpec(shape, lambda *_: (0,) * nd, pipeline_mode=pl.Buffered(1))


def _params(sem):
    return pltpu.CompilerParams(dimension_semantics=sem, vmem_limit_bytes=VMEM_LIMIT)


def _adaln_kernel(cond_ref, w_ref, b_ref, o_ref):
    cond = cond_ref[...]
    s = (cond * jax.nn.sigmoid(cond)).astype(BF16)
    o_ref[0] = _dot(s, w_ref[0].astype(BF16)) + b_ref[0]


def _adaln(cond, w_ada, b_ada):
    depth = w_ada.shape[0]
    rows = cond.shape[0]
    tn = ADALN_COLS
    return pl.pallas_call(
        _adaln_kernel,
        grid=(depth, 6 * D_MODEL // tn),
        in_specs=[
            pl.BlockSpec((rows, D_MODEL), lambda l, j: (0, 0)),
            pl.BlockSpec((1, D_MODEL, tn), lambda l, j: (l, 0, j)),
            pl.BlockSpec((1, 1, tn), lambda l, j: (l, 0, j)),
        ],
        out_specs=pl.BlockSpec((1, rows, tn), lambda l, j: (l, 0, j)),
        out_shape=jax.ShapeDtypeStruct((depth, rows, 6 * D_MODEL), F32),
        compiler_params=_params(("arbitrary", "arbitrary")),
        name="adaln",
    )(cond, w_ada, b_ada.reshape(depth, 1, 6 * D_MODEL))


def _sub_tiles(n_rows, size=PROJ_SUBTILE):
    return [slice(r0, r0 + size) for r0 in range(0, n_rows, size)]


def _modulated(x, shift, scale):
    ms = jnp.mean(x * x, axis=-1, keepdims=True)
    return x * lax.rsqrt(ms + EPS) * (1.0 + scale) + shift


def _proj_kernel(x_ref, sh_ref, sc_ref, cos_ref, sa_ref, sb_ref,
                 wuv_ref, wb_ref, wc_ref, wz_ref, w2_ref, b2_ref,
                 sgun_ref, ws_ref, bs_ref, qn_ref, kn_ref,
                 a_ref, q_ref, k_ref, v_ref, kq_ref, vc_ref, r_ref, g2_ref):
    ri = lax.broadcasted_iota(jnp.int32, (2 * LANES, 2 * LANES), 0) // HEAD_DIM
    ci = lax.broadcasted_iota(jnp.int32, (2 * LANES, 2 * LANES), 1) // HEAD_DIM
    group_ones = jnp.where(ri == ci, 1.0, 0.0).astype(BF16)

    tiles = _sub_tiles(x_ref.shape[1])

    hbs = {rows.start: _modulated(x_ref[0, rows], sh_ref[0], sc_ref[0]).astype(BF16) for rows in tiles}
    zs = {0: _dot(hbs[0], wz_ref[...]).astype(BF16)}

    def big(rows):
        hb = hbs[rows.start]
        t = dict(uv=_dot(hb, wuv_ref[...]))
        nxt = rows.stop
        if nxt in hbs:
            zs[nxt] = _dot(hbs[nxt], wz_ref[...]).astype(BF16)
        t["bq"] = _dot(hb, wb_ref[...])
        t["z"] = zs[rows.start]
        return t

    def mixer_c_projection(rows):
        cp = _dot(hbs[rows.start], wc_ref[...])
        n_q = GLA_HEADS * GLA_DK
        v_end = 2 * n_q + BRANCH_WIDTH
        kq_ref[0, rows, :n_q] = cp[:, :n_q] * GLA_DK ** -0.5
        kq_ref[0, rows, n_q:] = cp[:, n_q:2 * n_q]
        vc_ref[0, rows] = cp[:, 2 * n_q:v_end].astype(vc_ref.dtype)
        r_ref[0, rows] = cp[:, v_end:].astype(r_ref.dtype)

    def prepare(rows, t):
        bq, uv = t["bq"], t["uv"]
        v_ref[0, rows] = bq[:, K_END:].astype(v_ref.dtype)
        t["sq"] = [(xs * xs).astype(BF16) for xs in (bq[:, :2 * LANES], bq[:, 2 * LANES:Q_END], bq[:, Q_END:])]
        t["u"] = jax.nn.gelu(uv[:, :BRANCH_WIDTH])
        va = jax.nn.gelu(uv[:, BRANCH_WIDTH:])
        t["vg"] = []
        for g in range(SGU_GROUPS):
            lanes = slice(g * LANES, (g + 1) * LANES)
            vg = va[:, lanes]
            vg = vg * lax.rsqrt(jnp.mean(vg * vg, axis=-1, keepdims=True) + EPS) * sgun_ref[:, lanes]
            t["vg"].append(vg.astype(BF16))

    def small(rows, t):
        bq = t["bq"]
        cos, sa, sb = cos_ref[rows], sa_ref[rows], sb_ref[rows]

        def rope(y):
            return y * cos + pltpu.roll(y, HEAD_DIM // 4, 1) * sa + pltpu.roll(y, LANES - HEAD_DIM // 4, 1) * sb

        logit = _dot(t["z"], w2_ref[...]) + b2_ref[...]
        log_sig = jnp.minimum(logit, 0.0) - jnp.log(1.0 + jnp.exp(-jnp.abs(logit)))
        g2_ref[0, rows] = log_sig * (1.0 / GLA_NORMALIZER)
        inv = [lax.rsqrt(_dot(sq, group_ones) * (1.0 / HEAD_DIM) + EPS) for sq in t["sq"]]
        for s in range(BRANCH_WIDTH // LANES):
            lanes = slice(s * LANES, (s + 1) * LANES)
            qn = bq[:, lanes] * inv[s // 2][:, (s % 2) * LANES:(s % 2 + 1) * LANES] * qn_ref[...]
            q_ref[0, rows, lanes] = (rope(qn) * (HEAD_DIM ** -0.5 * LOG2E)).astype(q_ref.dtype)
        k_ref[0, rows] = rope(bq[:, Q_END:K_END] * inv[2][:, :LANES] * kn_ref[...]).astype(k_ref.dtype)
        n_chunk = (rows.stop - rows.start) // SGU_CHUNK
        for g in range(SGU_GROUPS):
            lanes = slice(g * LANES, (g + 1) * LANES)
            vg = jnp.concatenate([t["vg"][g][c * SGU_CHUNK:(c + 1) * SGU_CHUNK] for c in range(n_chunk)], axis=1)
            s_all = _dot(ws_ref[g], vg)
            for c in range(n_chunk):
                cr = slice(c * SGU_CHUNK, (c + 1) * SGU_CHUNK)
                out_rows = slice(rows.start + c * SGU_CHUNK, rows.start + (c + 1) * SGU_CHUNK)
                s = s_all[:, c * LANES:(c + 1) * LANES] + bs_ref[g]
                a_ref[0, out_rows, lanes] = (t["u"][cr, lanes] * s).astype(a_ref.dtype)

    prev = None
    for rows in tiles:
        cur = big(rows)
        if rows is not tiles[-1]:
            mixer_c_projection(rows)
        if prev is not None:
            small(*prev)
        prepare(rows, cur)
        prev = (rows, cur)
    small(*prev)
    mixer_c_projection(tiles[-1])


def _mod_spec(j, mod_row):
    if mod_row is None:
        return pl.BlockSpec((1, 1, D_MODEL), lambda b, i: (b * 6 + j, 0, 0))
    return pl.BlockSpec((1, 1, D_MODEL), lambda b, i: (mod_row * 6 + j, 0, 0))


def _proj(x, mods, mod_row, rope, lw, tm):
    B, N, _ = x.shape
    cos, sa, sb = rope
    tok = lambda w: pl.BlockSpec((1, tm, w), lambda b, i: (b, i, 0))
    tab = pl.BlockSpec((tm, LANES), lambda b, i: (i, 0))
    consts = [lw["w_uv"], lw["w_b"], lw["w_c"], lw["w_z"], lw["w2"], lw["b2"],
              lw["sgu_norm"], lw["w_sgu"], lw["b_sgu"], lw["q_norm"], lw["k_norm"]]
    out_w = [(512, BF16), (512, BF16), (128, BF16), (128, BF16), (512, F32), (512, BF16),
             (512, BF16), (512, F32)]
    return pl.pallas_call(
        _proj_kernel,
        grid=(B, N // tm),
        in_specs=[tok(D_MODEL), _mod_spec(0, mod_row), _mod_spec(1, mod_row), tab, tab, tab]
                 + [_const_spec(c.shape) for c in consts],
        out_specs=[tok(w) for w, _ in out_w],
        out_shape=[jax.ShapeDtypeStruct((B, N, w), dt) for w, dt in out_w],
        compiler_params=_params(("parallel", "parallel")),
        name="proj",
    )(x, mods, mods, cos, sa, sb, *consts)


def _attn_kernel(sink_ref, q_ref, *refs, local):
    blk = SWA_BLOCK
    n_q = q_ref.shape[1] // blk
    n_slab = BRANCH_WIDTH // LANES
    if local:
        band_ref, kp_ref, kc_ref, kn_ref, vp_ref, vc_ref, vn_ref, kx_ref, vx_ref, o_ref = refs
        k_loc = [kp_ref[0]] + [kc_ref[0, j * blk:(j + 1) * blk] for j in range(n_q)] + [kn_ref[0]]
        v_loc = [vp_ref[0]] + [vc_ref[0, j * blk:(j + 1) * blk] for j in range(n_q)] + [vn_ref[0]]
        keys = [jnp.concatenate(k_loc[j:j + 3] + [kx_ref[0]], axis=0) for j in range(n_q)]
        vals = [jnp.concatenate(v_loc[j:j + 3] + [vx_ref[0]], axis=0) for j in range(n_q)]
        i = pl.program_id(1)
        no_prev = jnp.where(i == 0, NEG_BIG, 0.0)
        no_next = jnp.where(i == pl.num_programs(1) - 1, NEG_BIG, 0.0)
        bias_prev = [band_ref[:blk] + no_prev if j == 0 else band_ref[:blk] for j in range(n_q)]
        bias_next = [band_ref[blk:] + no_next if j == n_q - 1 else band_ref[blk:] for j in range(n_q)]
    else:
        kx_ref, vx_ref, o_ref = refs
        keys, vals = [kx_ref[0]] * n_q, [vx_ref[0]] * n_q
    head0_lanes = lax.broadcasted_iota(jnp.int32, (blk, LANES), 1) < HEAD_DIM
    val_lane = lax.broadcasted_iota(jnp.int32, vals[0].shape, 1)
    col_slab = lax.broadcasted_iota(jnp.int32, (1, n_slab * blk), 1) // blk
    head0_rows = lax.broadcasted_iota(jnp.int32, (LANES, n_slab * blk), 0) < HEAD_DIM

    scores = {}

    def score_block(j):
        rows = slice(j * blk, (j + 1) * blk)
        for h in range(SWA_KV_HEADS):
            own = head0_lanes if h == 0 else ~head0_lanes
            qh = [jnp.where(own, q_ref[0, rows, s * LANES:(s + 1) * LANES], 0.0).astype(BF16)
                  for s in range(n_slab)]
            scores[j, h] = _dot_nt(keys[j], jnp.concatenate(qh, axis=0))

    score_block(0)
    for j in range(n_q):
        if j + 1 < n_q:
            score_block(j + 1)
        out_t = []
        for h in range(SWA_KV_HEADS):
            sc = scores[j, h]
            if local:
                sc = jnp.concatenate([sc[:blk] + bias_prev[j], sc[blk:2 * blk],
                                      sc[2 * blk:3 * blk] + bias_next[j], sc[3 * blk:]], axis=0)
            sink = jnp.zeros((1, n_slab * blk), F32)
            for s in range(n_slab):
                sink = jnp.where(col_slab == s, sink_ref[h * SWA_GROUP + s] * LOG2E, sink)
            m = jnp.maximum(jnp.max(sc, axis=0, keepdims=True), sink)
            p = jnp.exp2(sc - m).astype(BF16)
            own = (val_lane < HEAD_DIM) if h == 0 else (val_lane >= HEAD_DIM)
            pv = _dot_tn(jnp.where(own, vals[j], jnp.ones_like(vals[j])), p)
            sum_row = HEAD_DIM if h == 0 else 0
            out_t.append(pv / (pv[sum_row:sum_row + 1] + jnp.exp2(sink - m)))
        o_t = jnp.where(head0_rows, out_t[0], out_t[1])
        for s in range(n_slab):
            o_ref[0, j * blk:(j + 1) * blk, s * LANES:(s + 1) * LANES] = (
                o_t[:, s * blk:(s + 1) * blk].T.astype(o_ref.dtype))


def _band_bias(blk):
    kj = np.arange(blk)[:, None]
    qi = np.arange(blk)[None, :]
    prev_ok = np.abs(kj - blk - qi) <= SWA_WINDOW
    next_ok = np.abs(kj + blk - qi) <= SWA_WINDOW
    band = np.where(np.concatenate([prev_ok, next_ok], axis=0), 0.0, NEG_BIG)
    return jnp.asarray(np.tile(band, (1, BRANCH_WIDTH // LANES)), F32)


def _attn(q, k, v, k_ctx, v_ctx, sink, local):
    B, N, _ = q.shape
    C = k_ctx.shape[1]
    blk = SWA_BLOCK
    nb = N // blk
    n_q = min(ATTN_Q_BLOCKS, nb)
    steps = nb // n_q
    qspec = pl.BlockSpec((1, n_q * blk, BRANCH_WIDTH), lambda b, i: (b, i, 0))
    ctx_spec = pl.BlockSpec((1, C, LANES), lambda b, i: (b, 0, 0))
    smem = pl.BlockSpec(memory_space=pltpu.SMEM)
    if local:
        prev = pl.BlockSpec((1, blk, LANES), lambda b, i: (b, jnp.maximum(n_q * i - 1, 0), 0))
        cur = pl.BlockSpec((1, n_q * blk, LANES), lambda b, i: (b, i, 0))
        nxt = pl.BlockSpec((1, blk, LANES), lambda b, i: (b, jnp.minimum(n_q * (i + 1), nb - 1), 0))
        band = _band_bias(blk)
        in_specs = [smem, qspec, _const_spec(band.shape), prev, cur, nxt, prev, cur, nxt, ctx_spec, ctx_spec]
        args = (sink, q, band, k, k, k, v, v, v, k_ctx, v_ctx)
    else:
        in_specs = [smem, qspec, ctx_spec, ctx_spec]
        args = (sink, q, k_ctx, v_ctx)
    return pl.pallas_call(
        functools.partial(_attn_kernel, local=local),
        grid=(B, steps),
        in_specs=in_specs,
        out_specs=qspec,
        out_shape=jax.ShapeDtypeStruct((B, N, BRANCH_WIDTH), BF16),
        compiler_params=_params(("parallel", "parallel")),
        name="attn_local" if local else "attn_ctx",
    )(*args)


def _gla_kernel(kqf_ref, vf_ref, gf_ref, kqb_ref, vb_ref, gb_ref, s0_ref,
                of_ref, ob_ref, sfin_ref, st_ref):
    i = pl.program_id(1)
    T = kqf_ref.shape[1]
    n_chunk = T // GLA_CHUNK

    n_seq = kqf_ref.shape[0]

    @pl.when(i == 0)
    def _():
        st_ref[...] = s0_ref[...]

    cum_rows = min(T, GLA_CUM_ROWS)
    ri = lax.broadcasted_iota(jnp.int32, (cum_rows, cum_rows), 0)
    ci = lax.broadcasted_iota(jnp.int32, (cum_rows, cum_rows), 1)
    same_chunk = (ri // GLA_CHUNK) == (ci // GLA_CHUNK)
    li = lax.broadcasted_iota(jnp.int32, (GLA_CHUNK, 2 * GLA_CHUNK), 0)
    mi = lax.broadcasted_iota(jnp.int32, (GLA_CHUNK, 2 * GLA_CHUNK), 1) % GLA_CHUNK
    first_head_lanes = lax.broadcasted_iota(jnp.int32, (GLA_CHUNK, LANES), 1) < GLA_DK
    own_block = ((lax.broadcasted_iota(jnp.int32, (2 * GLA_DV, LANES), 0) < GLA_DV)
                 == (lax.broadcasted_iota(jnp.int32, (2 * GLA_DV, LANES), 1) < GLA_DK))

    dirs = ((kqf_ref, vf_ref, gf_ref, of_ref), (kqb_ref, vb_ref, gb_ref, ob_ref))
    cums = {}
    for d, (kq_ref, v_ref, g_ref, o_ref) in enumerate(dirs):
        tri_full = jnp.where(same_chunk & ((ci >= ri) if d == 1 else (ci <= ri)), 1.0, 0.0).astype(BF16)
        for b in range(n_seq):
            cums[b, d] = jnp.concatenate([_split_dot_left(tri_full, g_ref[b, r0:r0 + cum_rows])
                                          for r0 in range(0, T, cum_rows)], axis=0)

    def chunk_of(d, step):
        return n_chunk - 1 - step if d == 1 else step

    def operands(b, d, p, c, anchor):
        kq_ref = dirs[d][0]
        lanes = slice(p * LANES, (p + 1) * LANES)
        k_lanes = slice(GLA_HEADS * GLA_DK + p * LANES, GLA_HEADS * GLA_DK + (p + 1) * LANES)
        rows = slice(c * GLA_CHUNK, (c + 1) * GLA_CHUNK)
        ref_row = GLA_CHUNK // 2 - 1 if d == 1 else GLA_CHUNK // 2
        tot_row = 0 if d == 1 else GLA_CHUNK - 1
        bc = cums[b, d][rows, lanes]
        if anchor is not None:
            bc = bc + anchor
        q2 = kq_ref[b, rows, lanes]
        k2 = kq_ref[b, rows, k_lanes]
        b_ref = bc[ref_row:ref_row + 1]
        b_tot = bc[tot_row:tot_row + 1]
        ke = k2 * jnp.exp(b_ref - bc)
        return dict(
            qe=(q2 * jnp.exp(bc - b_ref)).astype(BF16),
            q_in=(q2 * jnp.exp(bc)).astype(BF16),
            k_out=(k2 * jnp.exp(b_tot - bc)).astype(BF16),
            decay=jnp.exp(b_tot),
            ke=jnp.concatenate([jnp.where(first_head_lanes, ke, 0.0),
                                jnp.where(first_head_lanes, 0.0, ke)], axis=0).astype(BF16))

    ops = {}
    anchors = {}

    def score_group(gi, steps):
        anchor = anchors.get(gi - 2)
        for step in steps:
            for b in range(n_seq):
                for d in range(2):
                    for p in range(GLA_HEADS // 2):
                        u = operands(b, d, p, chunk_of(d, step), anchor)
                        keep = (mi >= li) if d == 1 else (mi <= li)
                        scores = _dot_nt(u["qe"], u["ke"])
                        u["pm"] = jnp.where(keep, scores, 0.0).astype(BF16)
                        ops[b, d, p, step] = u
        anchors[gi] = jnp.minimum(jnp.abs(scores[:1]), 0.0)

    zeros_v = jnp.zeros((GLA_CHUNK, GLA_DV), BF16)

    def scan_group(steps):
        for step, b, (d, (kq_ref, v_ref, g_ref, o_ref)), p in itertools.product(
                steps, range(n_seq), enumerate(dirs), range(GLA_HEADS // 2)):
            c = chunk_of(d, step)
            rows = slice(c * GLA_CHUNK, (c + 1) * GLA_CHUNK)
            lanes = slice(p * 2 * GLA_DV, (p + 1) * 2 * GLA_DV)
            u = ops.pop((b, d, p, step))
            v2 = v_ref[b, rows, lanes]
            v_diag = jnp.concatenate(
                [jnp.concatenate([v2[:, :GLA_DV], zeros_v], axis=1),
                 jnp.concatenate([zeros_v, v2[:, GLA_DV:]], axis=1)], axis=0)
            st = st_ref[b, d, p]
            o = _dot(u["pm"], v_diag) + _dot_nt(u["q_in"], st.astype(BF16))
            o_ref[b, rows, lanes] = o.astype(o_ref.dtype)
            st_ref[b, d, p] = st * u["decay"] + jnp.where(own_block, _dot_tn(v2, u["k_out"]), 0.0)

    groups = [range(s, min(s + GLA_GROUP_STEPS, n_chunk)) for s in range(0, n_chunk, GLA_GROUP_STEPS)]
    score_group(0, groups[0])
    for gi, steps in enumerate(groups):
        if gi + 1 < len(groups):
            score_group(gi + 1, groups[gi + 1])
        scan_group(steps)

    @pl.when(i == pl.num_programs(1) - 1)
    def _():
        sfin_ref[...] = st_ref[...]


def _split_dot_left(m, a):
    hi = a.astype(BF16)
    lo = (a - hi.astype(F32)).astype(BF16)
    return _dot(m, hi) + _dot(m, lo)


def _gla(kq, v, g2, s0, T):
    B, N, _ = kq.shape
    nt = N // T
    nb = GLA_SEQS_PER_STEP
    fwd = lambda w: pl.BlockSpec((nb, T, w), lambda b, i: (b, i, 0))
    bwd = lambda w: pl.BlockSpec((nb, T, w), lambda b, i: (b, nt - 1 - i, 0))
    g_fwd = pl.BlockSpec((nb, T, 256), lambda b, i: (b, i, 0))
    g_bwd = pl.BlockSpec((nb, T, 256), lambda b, i: (b, nt - 1 - i, 1))
    st_shape = GLA_STATE_SHAPE
    st_spec = pl.BlockSpec((nb,) + st_shape, lambda b, i: (b, 0, 0, 0, 0))
    return pl.pallas_call(
        _gla_kernel,
        grid=(B // nb, nt),
        in_specs=[fwd(512), fwd(512), g_fwd, bwd(512), bwd(512), g_bwd, st_spec],
        out_specs=[fwd(512), bwd(512), st_spec],
        out_shape=[jax.ShapeDtypeStruct((B, N, 512), BF16), jax.ShapeDtypeStruct((B, N, 512), BF16),
                   jax.ShapeDtypeStruct((B,) + st_shape, F32)],
        scratch_shapes=[pltpu.VMEM((nb,) + st_shape, F32)],
        compiler_params=_params(("parallel", "arbitrary")),
        name="gla",
    )(kq, v, g2, kq, v, g2, s0)


def _merge_kernel(x_ref, sh_ref, sc_ref, g1_ref, a_ref, b_ref, of_ref, ob_ref, r_ref,
                  gn_ref, wg_ref, wbr_ref, wo_ref, o_ref):
    def gated_mix(rows):
        hb = _modulated(x_ref[0, rows], sh_ref[0], sc_ref[0]).astype(BF16)
        o = of_ref[0, rows].astype(F32) + ob_ref[0, rows].astype(F32)
        cs = []
        for h in range(GLA_HEADS):
            oh = o[:, h * LANES:(h + 1) * LANES]
            cs.append(oh * lax.rsqrt(jnp.mean(oh * oh, axis=-1, keepdims=True) + EPS) * gn_ref[...])
        r = r_ref[0, rows].astype(F32)
        c_out = (jnp.concatenate(cs, axis=1) * (r * jax.nn.sigmoid(r))).astype(BF16)
        branches = (a_ref[0, rows], b_ref[0, rows], c_out)
        gates = [jax.nn.sigmoid(_dot(hb, wg_ref[:, k * D_MODEL:(k + 1) * D_MODEL])) for k in range(N_BRANCH)]
        mixed = None
        for k in range(N_BRANCH):
            term = gates[k] * _dot(branches[k], wbr_ref[k])
            mixed = term if mixed is None else mixed + term
        return mixed.astype(BF16)

    def project_out(rows, mixed):
        o_ref[0, rows] = x_ref[0, rows] + g1_ref[0] * _dot(mixed, wo_ref[...])

    prev = None
    for rows in _sub_tiles(x_ref.shape[1], MERGE_SUBTILE):
        mixed = gated_mix(rows)
        if prev is not None:
            project_out(*prev)
        prev = (rows, mixed)
    project_out(*prev)


def _merge(x, mods, mod_row, a, b, o_f, o_b, r, lw, tm):
    B, N, _ = x.shape
    tok = lambda w: pl.BlockSpec((1, tm, w), lambda b_, i: (b_, i, 0))
    consts = [lw["gla_norm"], lw["w_g"], lw["w_br"], lw["w_o"]]
    return pl.pallas_call(
        _merge_kernel,
        grid=(B, N // tm),
        in_specs=[tok(D_MODEL)] + [_mod_spec(j, mod_row) for j in range(3)] + [tok(512)] * 5
                 + [_const_spec(c.shape) for c in consts],
        out_specs=tok(D_MODEL),
        out_shape=jax.ShapeDtypeStruct((B, N, D_MODEL), F32),
        compiler_params=_params(("parallel", "parallel")),
        name="merge",
    )(x, mods, mods, mods, a, b, o_f, o_b, r, *consts)


def _ffn_kernel(x_ref, sh_ref, sc_ref, g2_ref, w1_ref, w2_ref, o_ref):
    x = x_ref[0]
    hb = _modulated(x, sh_ref[0], sc_ref[0]).astype(BF16)
    t = jnp.maximum(_dot(hb, w1_ref[...]), 0.0)
    f = _dot((t * t).astype(BF16), w2_ref[...])
    o_ref[0] = x + g2_ref[0] * f


def _ffn(x, mods, mod_row, lw, tm):
    B, N, _ = x.shape
    tok = pl.BlockSpec((1, tm, D_MODEL), lambda b, i: (b, i, 0))
    return pl.pallas_call(
        _ffn_kernel,
        grid=(B, N // tm),
        in_specs=[tok] + [_mod_spec(j, mod_row) for j in (3, 4, 5)]
                 + [_const_spec(lw["w_ff1"].shape), _const_spec(lw["w_ff2"].shape)],
        out_specs=tok,
        out_shape=jax.ShapeDtypeStruct((B, N, D_MODEL), F32),
        compiler_params=_params(("parallel", "parallel")),
        name="ffn",
    )(x, mods, mods, mods, lw["w_ff1"], lw["w_ff2"])


def _rope_tables(n_tokens):
    lane = np.arange(LANES)
    d = lane % HEAD_DIM
    use_col = (d // (HEAD_DIM // 2)) == 1
    second = ((d % (HEAD_DIM // 2)) // (HEAD_DIM // 4)) == 1
    f = d % (HEAD_DIM // 4)
    n_freq = HEAD_DIM // 4
    freqs = ROPE_BASE ** (-jnp.arange(n_freq, dtype=F32) / n_freq)
    t = jnp.arange(n_tokens)
    row = (t // GRID_W).astype(F32)
    col = (t % GRID_W).astype(F32)
    ang_row = row[:, None] * freqs[None, :]
    ang_col = col[:, None] * freqs[None, :]
    ang = jnp.where(jnp.asarray(use_col)[None, :], ang_col[:, f], ang_row[:, f])
    cos, sin = jnp.cos(ang), jnp.sin(ang)
    second = jnp.asarray(second)[None, :]
    return cos, jnp.where(second, sin, 0.0), jnp.where(second, 0.0, -sin)


def _identity_tables(n_tokens):
    return (jnp.ones((n_tokens, LANES), F32), jnp.zeros((n_tokens, LANES), F32),
            jnp.zeros((n_tokens, LANES), F32))


def _layer_weights(l, w_in, q_norm, k_norm, sgu_norm, w_sgu, b_sgu, w_gate_f, b_gate_f,
                   w_gate_b, b_gate_b, gla_norm, w_br, w_o, w_ff1, w_ff2):
    w = w_in[l]
    col = lambda i: w[:, _OFFS[i]:_OFFS[i + 1]]
    wq = col(2).reshape(D_MODEL, SWA_KV_HEADS, SWA_GROUP, HEAD_DIM).transpose(0, 2, 1, 3).reshape(D_MODEL, 512)
    w_z = jnp.zeros((D_MODEL, LANES), F32).at[:, :2 * GLA_RANK].set(w[:, _OFFS[9]:_OFFS[11]])
    n_dk = GLA_HEADS * GLA_DK
    w2 = jnp.zeros((LANES, 2 * n_dk), F32)
    w2 = w2.at[:GLA_RANK, :n_dk].set(w_gate_f[l]).at[GLA_RANK:2 * GLA_RANK, n_dk:].set(w_gate_b[l])
    b2 = jnp.concatenate([b_gate_f[l], b_gate_b[l]])[None, :]
    wbr1 = w_br[l, 1].reshape(SWA_KV_HEADS, SWA_GROUP, HEAD_DIM, D_MODEL).transpose(1, 0, 2, 3).reshape(512, D_MODEL)
    return {
        "w_uv": w[:, _OFFS[0]:_OFFS[2]].astype(BF16),
        "w_b": jnp.concatenate([wq, col(3), col(4)], axis=1).astype(BF16),
        "w_c": w[:, _OFFS[5]:_OFFS[9]].astype(BF16),
        "w_z": w_z.astype(BF16),
        "w_g": col(11).astype(BF16),
        "w2": w2.astype(BF16),
        "b2": b2,
        "sgu_norm": sgu_norm[l][None, :],
        "w_sgu": w_sgu[l].astype(BF16),
        "b_sgu": jnp.broadcast_to(b_sgu[l][:, :, None], (SGU_GROUPS, SGU_CHUNK, LANES)),
        "q_norm": jnp.tile(q_norm[l], 2)[None, :],
        "k_norm": jnp.tile(k_norm[l], 2)[None, :],
        "gla_norm": gla_norm[l][None, :],
        "w_br": jnp.stack([w_br[l, 0], wbr1, w_br[l, 2]]).astype(BF16),
        "w_o": w_o[l].astype(BF16),
        "w_ff1": w_ff1[l].astype(BF16),
        "w_ff2": w_ff2[l].astype(BF16),
    }


def kernel(x, c, ctx, c_ctx, w_ada, b_ada, w_in, q_norm, k_norm, sink, sgu_norm, w_sgu, b_sgu,
           w_gate_f, b_gate_f, w_gate_b, b_gate_b, gla_norm, w_br, w_o, w_ff1, w_ff2):
    B, N, _ = x.shape
    C = ctx.shape[1]
    depth = w_ada.shape[0]
    n_cond = 16
    tm_lat = min(TOKEN_BLOCK, N)
    tm_ctx = min(TOKEN_BLOCK, B * C)
    gla_tile = min(GLA_TILE, N)

    cond = jnp.zeros((n_cond, D_MODEL), F32).at[:B].set(c).at[B].set(c_ctx)
    mods = _adaln(cond, w_ada, b_ada)
    rope_lat = _rope_tables(N)
    rope_ctx = _identity_tables(B * C)
    zero_state = jnp.zeros((B,) + GLA_STATE_SHAPE, F32)

    flat = lambda t: t.reshape(1, B * C, t.shape[-1])
    per_batch = lambda t: t.reshape(B, C, t.shape[-1])
    ctx = flat(ctx)

    for l in range(depth):
        lw = _layer_weights(l, w_in, q_norm, k_norm, sgu_norm, w_sgu, b_sgu, w_gate_f, b_gate_f,
                            w_gate_b, b_gate_b, gla_norm, w_br, w_o, w_ff1, w_ff2)
        m = mods[l].reshape(n_cond * 6, 1, D_MODEL)

        (ca, cq, ck, cv, ckq, cvc, cr, cg2) = _proj(ctx, m, B, rope_ctx, lw, tm_ctx)
        ck, cv = per_batch(ck), per_batch(cv)
        oc_f, oc_b, s_ctx = _gla(per_batch(ckq), per_batch(cvc), per_batch(cg2), zero_state, min(gla_tile, C))

        (a, q, k, v, kq, vc, r, g2) = _proj(x, m, None, rope_lat, lw, tm_lat)
        b_out = _attn(q, k, v, ck, cv, sink[l], local=True)
        o_f, o_b, _ = _gla(kq, vc, g2, s_ctx, gla_tile)
        x_new = _merge(x, m, None, a, b_out, o_f, o_b, r, lw, tm_lat)
        x_new = _ffn(x_new, m, None, lw, tm_lat)

        if l < depth - 1:
            cb_out = _attn(per_batch(cq), None, None, ck, cv, sink[l], local=False)
            ctx_new = _merge(ctx, m, B, ca, flat(cb_out), flat(oc_f), flat(oc_b), cr, lw, tm_ctx)
            ctx = _ffn(ctx_new, m, B, lw, tm_ctx)
        x = x_new

    return x
```
